```python
import jax, jax.numpy as jnp
from jax import lax
import numpy as np

D_MODEL = 4096
BATCH = 4
SEQ = 2048
DEPTH = 2
DEC_BATCH = 32
DEC_SEQ = 8
PAST_LEN = 16384
PAGE_SIZE = 128

HEAD_DIM = 128
N_HEADS_A = D_MODEL // 256
N_KV_A = max(1, N_HEADS_A // 4)
GROUP_A = N_HEADS_A // N_KV_A
WINDOW_A = 128
DILATED_PAIRS = ((128, 1), (512, 4), (2048, 16))
N_HEADS_B = D_MODEL // 512
D_FF = 256 * ((8 * D_MODEL // 3 + 255) // 256)
BLOCK = 128
ROPE_THETA = 10000.0
EPS = 1e-5
W_QA = N_HEADS_A * HEAD_DIM
W_KA = N_KV_A * HEAD_DIM
W_B = N_HEADS_B * HEAD_DIM
IN_SIZES = (W_QA, W_KA, W_KA) + (W_B,) * (3 * len(DILATED_PAIRS))
IN_WIDTH = sum(IN_SIZES)

kernel_name = 'hybrid_swa_sink_dilated_macaron_step'


def rmsnorm(x, g):
    xf = x.astype(jnp.float32)
    y = xf * lax.rsqrt(jnp.mean(xf * xf, axis=-1, keepdims=True) + EPS)
    return (y * g.astype(jnp.float32)).astype(x.dtype)


def swiglu(h, w_in, w_out):
    a, b = jnp.split(h @ w_in, 2, axis=-1)
    return (jax.nn.silu(a) * b) @ w_out


def rope(x, pos):
    half = HEAD_DIM // 2
    inv = ROPE_THETA ** (-jnp.arange(half, dtype=jnp.float32) / half)
    ang = pos.astype(jnp.float32)[:, None] * inv[None, :]
    cos, sin = jnp.cos(ang)[:, None, :], jnp.sin(ang)[:, None, :]
    xf = x.astype(jnp.float32)
    x1, x2 = xf[..., :half], xf[..., half:]
    return jnp.concatenate([x1 * cos - x2 * sin, x2 * cos + x1 * sin], axis=-1).astype(x.dtype)


def project(h, w_in, pos):
    n, t, _ = h.shape
    parts = jnp.split(h @ w_in, np.cumsum(IN_SIZES)[:-1].tolist(), axis=-1)
    hd = [p.reshape(n, t, -1, HEAD_DIM) for p in parts]
    q_a, k_a, v_a = rope(hd[0], pos), rope(hd[1], pos), hd[2]
    groups = [(rope(hd[3 + 3 * g], pos), rope(hd[4 + 3 * g], pos), hd[5 + 3 * g])
              for g in range(len(DILATED_PAIRS))]
    return q_a, k_a, v_a, groups


def masked_softmax(s, valid, sink=None):
    s = jnp.where(valid, s, -jnp.inf)
    m = jnp.max(s, axis=-1, keepdims=True)
    if sink is not None:
        m = jnp.maximum(m, sink)
    e = jnp.exp(s - m)
    l = jnp.sum(e, axis=-1, keepdims=True)
    denom = l if sink is None else l + jnp.exp(sink - m)
    return e / denom, (m + jnp.log(l))[..., 0]


def banded_attention(q, k, v, max_dist, sink=None):
    n, L, kvh, grp, hd = q.shape
    nb = L // BLOCK
    qb = q.reshape(n, nb, BLOCK, kvh, grp, hd)

    def windows(t):
        t = jnp.concatenate([jnp.zeros_like(t[:, :BLOCK]), t], axis=1).reshape(n, nb + 1, BLOCK, kvh, hd)
        return jnp.concatenate([t[:, :-1], t[:, 1:]], axis=2)

    kw, vw = windows(k), windows(v)
    s = jnp.einsum('nbqkgd,nbjkd->nbkgqj', qb, kw).astype(jnp.float32) * (hd ** -0.5)
    kj = jnp.arange(2 * BLOCK)[None, :]
    dist = jnp.arange(BLOCK)[:, None] + BLOCK - kj
    band = (dist >= 0) & (dist <= max_dist)
    before_start = (jnp.arange(nb) == 0)[:, None, None] & (kj < BLOCK)[None]
    valid = (band[None] & ~before_start)[None, :, None, None]
    sk = None if sink is None else sink[None, None, :, :, None, None]
    p, lse = masked_softmax(s, valid, sk)
    o = jnp.einsum('nbkgqj,nbjkd->nbqkgd', p.astype(v.dtype), vw).reshape(n, L, kvh, grp, hd)
    lse = lse.transpose(0, 1, 4, 2, 3).reshape(n, L, kvh, grp)
    return o, lse


def dilated_prompt(q, k, v, window, dil):
    b, S, h, hd = q.shape
    L = S // dil
    Lp = -(-L // BLOCK) * BLOCK

    def to_sub(t):
        t = t.reshape(b, L, dil, h, hd).transpose(0, 2, 1, 3, 4).reshape(b * dil, L, h, hd)
        return jnp.pad(t, ((0, 0), (0, Lp - L), (0, 0), (0, 0)))

    o, lse = banded_attention(to_sub(q)[:, :, :, None], to_sub(k), to_sub(v), window // dil)
    o = o[:, :L, :, 0].reshape(b, dil, L, h, hd).transpose(0, 2, 1, 3, 4).reshape(b, S, h, hd)
    lse = lse[:, :L, :, 0].reshape(b, dil, L, h).transpose(0, 2, 1, 3).reshape(b, S, h)
    return o, lse


def dilated_sample(q, kv_full, lb, window, dil):
    T = q.shape[1]
    n_keys = window // dil + 1
    idx = lb + jnp.arange(T)[:, None] - dil * jnp.arange(n_keys)[None, :]
    valid = idx >= 0
    g = kv_full[:, jnp.clip(idx, 0)]
    s = jnp.einsum('bthd,btmhd->bhtm', q, g[:, :, :, 0]).astype(jnp.float32) * (HEAD_DIM ** -0.5)
    p, lse = masked_softmax(s, valid)
    o = jnp.einsum('bhtm,btmhd->bthd', p.astype(kv_full.dtype), g[:, :, :, 1])
    return o, lse.transpose(0, 2, 1)


def swa_sample(q, kv_full, sink):
    T, J = q.shape[1], kv_full.shape[1]
    lb = J - T
    s = jnp.einsum('btkgd,bjkd->bkgtj', q, kv_full[:, :, 0]).astype(jnp.float32) * (HEAD_DIM ** -0.5)
    dist = (lb + jnp.arange(T))[:, None] - jnp.arange(J)[None, :]
    valid = (dist >= 0) & (dist < WINDOW_A)
    p, _ = masked_softmax(s, valid, sink[None, :, :, None, None])
    return jnp.einsum('bkgtj,bjkd->btkgd', p.astype(kv_full.dtype), kv_full[:, :, 1])


def combine_dilated(outs, lses):
    w = jax.nn.softmax(jnp.stack(lses).astype(jnp.float32), axis=0)
    o = jnp.sum(w[..., None] * jnp.stack(outs).astype(jnp.float32), axis=0)
    return o.astype(outs[0].dtype)


def merge(h, o_a, o_b, w_gate, w_up_a, w_up_b, w_o):
    g_a, g_b = jnp.split(jax.nn.sigmoid(h @ w_gate), 2, axis=-1)
    return (g_a * (o_a @ w_up_a) + g_b * (o_b @ w_up_b)) @ w_o


def prompt_mixer(h, w_in, sink):
    n, S, _ = h.shape
    q_a, k_a, v_a, groups = project(h, w_in, jnp.arange(S))
    o_a, _ = banded_attention(q_a.reshape(n, S, N_KV_A, GROUP_A, HEAD_DIM), k_a, v_a, WINDOW_A - 1, sink)
    outs, lses, new_b = [], [], []
    for (window, dil), (q, k, v) in zip(DILATED_PAIRS, groups):
        o, lse = dilated_prompt(q, k, v, window, dil)
        outs.append(o)
        lses.append(lse)
        lw = min(window, S)
        new_b.append(jnp.stack([k[:, S - lw:], v[:, S - lw:]], axis=2))
    lw = min(WINDOW_A, S)
    new_a = jnp.stack([k_a[:, S - lw:], v_a[:, S - lw:]], axis=2)
    o_b = combine_dilated(outs, lses).reshape(n, S, W_B)
    return o_a.reshape(n, S, W_QA), o_b, new_a, new_b


def sample_mixer(h, w_in, sink, cache_a, caches_b):
    n, T, _ = h.shape
    q_a, k_a, v_a, groups = project(h, w_in, PAST_LEN + jnp.arange(T))
    new_a = jnp.stack([k_a, v_a], axis=2)
    o_a = swa_sample(q_a.reshape(n, T, N_KV_A, GROUP_A, HEAD_DIM),
                     jnp.concatenate([cache_a, new_a], axis=1), sink)
    outs, lses, new_b = [], [], []
    for (window, dil), (q, k, v), cache in zip(DILATED_PAIRS, groups, caches_b):
        new = jnp.stack([k, v], axis=2)
        o, lse = dilated_sample(q, jnp.concatenate([cache, new], axis=1), cache.shape[1], window, dil)
        outs.append(o)
        lses.append(lse)
        new_b.append(new)
    o_b = combine_dilated(outs, lses).reshape(n, T, W_B)
    return o_a.reshape(n, T, W_QA), o_b, new_a, new_b


def setup_inputs(seed: int = 0) -> dict:
    key = jax.random.key(seed)
    ks = iter(jax.random.split(key, 24))

    def nrm(shape, scale=1.0):
        return jax.random.normal(next(ks), shape, jnp.float32) * scale

    def gain(shape):
        return 1.0 + 0.05 * jax.random.normal(next(ks), shape, jnp.float32)

    def kv_cache(window, heads):
        return nrm((DEPTH, DEC_BATCH, min(window, PAST_LEN), 2, heads, HEAD_DIM))

    return {
        'x_prompt': nrm((BATCH, SEQ, D_MODEL)),
        'x_sample': nrm((DEC_BATCH, DEC_SEQ, D_MODEL)),
        'cache_a_kv': kv_cache(WINDOW_A, N_KV_A),
        'cache_b1_kv': kv_cache(DILATED_PAIRS[0][0], N_HEADS_B),
        'cache_b2_kv': kv_cache(DILATED_PAIRS[1][0], N_HEADS_B),
        'cache_b3_kv': kv_cache(DILATED_PAIRS[2][0], N_HEADS_B),
        'norm_ffn1': gain((DEPTH, D_MODEL)),
        'w_ffn1_in': nrm((DEPTH, D_MODEL, 2 * D_FF), D_MODEL ** -0.5),
        'w_ffn1_out': nrm((DEPTH, D_FF, D_MODEL), D_FF ** -0.5),
        'norm_mix': gain((DEPTH, D_MODEL)),
        'w_in': nrm((DEPTH, D_MODEL, IN_WIDTH), D_MODEL ** -0.5),
        'sinks': nrm((DEPTH, N_HEADS_A), 0.5),
        'w_gate': nrm((DEPTH, D_MODEL, 2 * D_MODEL), D_MODEL ** -0.5),
        'w_up_a': nrm((DEPTH, W_QA, D_MODEL), W_QA ** -0.5),
        'w_up_b': nrm((DEPTH, W_B, D_MODEL), W_B ** -0.5),
        'w_o': nrm((DEPTH, D_MODEL, D_MODEL), D_MODEL ** -0.5),
        'norm_ffn2': gain((DEPTH, D_MODEL)),
        'w_ffn2_in': nrm((DEPTH, D_MODEL, 2 * D_FF), D_MODEL ** -0.5),
        'w_ffn2_out': nrm((DEPTH, D_FF, D_MODEL), D_FF ** -0.5),
        'norm_final': gain((D_MODEL,)),
    }


def reference(x_prompt, x_sample, cache_a_kv, cache_b1_kv, cache_b2_kv, cache_b3_kv,
              norm_ffn1, w_ffn1_in, w_ffn1_out, norm_mix, w_in, sinks, w_gate, w_up_a, w_up_b, w_o,
              norm_ffn2, w_ffn2_in, w_ffn2_out, norm_final):
    caches_b = (cache_b1_kv, cache_b2_kv, cache_b3_kv)
    xp, xs = x_prompt, x_sample
    a_p, a_s = [], []
    b_p = [[] for _ in DILATED_PAIRS]
    b_s = [[] for _ in DILATED_PAIRS]
    for l in range(DEPTH):
        sink = sinks[l].astype(jnp.float32).reshape(N_KV_A, GROUP_A)
        xp = xp + 0.5 * swiglu(rmsnorm(xp, norm_ffn1[l]), w_ffn1_in[l], w_ffn1_out[l])
        xs = xs + 0.5 * swiglu(rmsnorm(xs, norm_ffn1[l]), w_ffn1_in[l], w_ffn1_out[l])
        hp = rmsnorm(xp, norm_mix[l])
        hs = rmsnorm(xs, norm_mix[l])
        oa, ob, na, nbs = prompt_mixer(hp, w_in[l], sink)
        xp = xp + merge(hp, oa, ob, w_gate[l], w_up_a[l], w_up_b[l], w_o[l])
        a_p.append(na)
        for g in range(len(DILATED_PAIRS)):
            b_p[g].append(nbs[g])
        oa, ob, na, nbs = sample_mixer(hs, w_in[l], sink, cache_a_kv[l], [c[l] for c in caches_b])
        xs = xs + merge(hs, oa, ob, w_gate[l], w_up_a[l], w_up_b[l], w_o[l])
        a_s.append(na)
        for g in range(len(DILATED_PAIRS)):
            b_s[g].append(nbs[g])
        xp = xp + 0.5 * swiglu(rmsnorm(xp, norm_ffn2[l]), w_ffn2_in[l], w_ffn2_out[l])
        xs = xs + 0.5 * swiglu(rmsnorm(xs, norm_ffn2[l]), w_ffn2_in[l], w_ffn2_out[l])
    y_prompt = rmsnorm(xp, norm_final)
    y_sample = rmsnorm(xs, norm_final)
    return (y_prompt, y_sample, jnp.stack(a_p), jnp.stack(a_s),
            jnp.stack(b_p[0]), jnp.stack(b_s[0]), jnp.stack(b_p[1]), jnp.stack(b_s[1]),
            jnp.stack(b_p[2]), jnp.stack(b_s[2]))
```

```python
import functools

import jax
import jax.numpy as jnp
from jax import lax
from jax.experimental import pallas as pl
from jax.experimental.pallas import tpu as pltpu

HEAD_DIM = 128
BLOCK = 128
WINDOW_A = 128
PAST_LEN = 16384
DILATED_PAIRS = ((128, 1), (512, 4), (2048, 16))
ROPE_THETA = 10000.0
EPS = 1e-5
NEG = -1e30
VMEM_LIMIT = 56 * 1024 * 1024

F32 = jnp.float32
BF16 = jnp.bfloat16


def _params(n_axes, vmem=VMEM_LIMIT):
    return pltpu.CompilerParams(dimension_semantics=("arbitrary",) * n_axes, vmem_limit_bytes=vmem)


def _dot(a, b):
    return jnp.dot(a, b, preferred_element_type=F32)


def _dot_nt(a, b):
    return lax.dot_general(a, b, (((1,), (1,)), ((), ())), preferred_element_type=F32)


def _rmsnorm_kernel(x_ref, g_ref, o_ref):
    x = x_ref[...]
    y = x * lax.rsqrt(jnp.mean(x * x, axis=-1, keepdims=True) + EPS)
    o_ref[...] = (y * g_ref[...]).astype(o_ref.dtype)


def _rmsnorm(x, g, out_dtype, tm):
    m, d = x.shape
    return pl.pallas_call(
        _rmsnorm_kernel,
        grid=(m // tm,),
        in_specs=[pl.BlockSpec((tm, d), lambda i: (i, 0)), pl.BlockSpec((1, d), lambda i: (0, 0))],
        out_specs=pl.BlockSpec((tm, d), lambda i: (i, 0)),
        out_shape=jax.ShapeDtypeStruct((m, d), out_dtype),
        compiler_params=_params(1),
        name="rmsnorm",
    )(x, g.reshape(1, d))


def _mm_kernel(*refs, n_a, b_to_a, n_extra, epilogue):
    n_b = len(b_to_a)
    a_refs = refs[:n_a]
    b_refs = refs[n_a:n_a + n_b]
    extra_refs = refs[n_a + n_b:n_a + n_b + n_extra]
    o_ref = refs[n_a + n_b + n_extra]
    w_refs = refs[n_a + n_b + n_extra + 1:]

    @pl.when(pl.program_id(1) == 0)
    def _():
        for b_ref, w_ref in zip(b_refs, w_refs):
            w_ref[...] = b_ref[...].astype(BF16)

    accs = [_dot(a_refs[ai][...], w_refs[k][...]) for k, ai in enumerate(b_to_a)]
    epilogue(accs, extra_refs, o_ref, pl.program_id(0))


def _matmul(a_list, b_list, extras, epilogue, n_out, out_dtype, tm, tn, name):
    m = a_list[0].shape[0]
    assert m % tm == 0 and n_out % tn == 0
    in_specs = [pl.BlockSpec((tm, a.shape[1]), lambda j, i: (i, 0)) for a in a_list]
    for w, layer, ai, off in b_list:
        assert w.shape[1] == a_list[ai].shape[1] and w.shape[2] % tn == 0
        in_specs.append(pl.BlockSpec((None, w.shape[1], tn), lambda j, i, layer=layer, off=off: (layer, 0, j + off)))
    for arr, shape, imap in extras:
        in_specs.append(pl.BlockSpec(shape, imap))
    kernel = functools.partial(_mm_kernel, n_a=len(a_list), b_to_a=tuple(b[2] for b in b_list),
                               n_extra=len(extras), epilogue=epilogue)
    return pl.pallas_call(
        kernel,
        grid=(n_out // tn, m // tm),
        in_specs=in_specs,
        out_specs=pl.BlockSpec((tm, tn), lambda j, i: (i, j)),
        out_shape=jax.ShapeDtypeStruct((m, n_out), out_dtype),
        scratch_shapes=[pltpu.VMEM((b[0].shape[1], tn), BF16) for b in b_list],
        compiler_params=_params(2),
        name=name,
    )(*a_list, *[b[0] for b in b_list], *[arr for arr, _, _ in extras])


def _swiglu_epilogue(accs, extra_refs, o_ref, j):
    a, b = accs
    o_ref[...] = (a * jax.nn.sigmoid(a) * b).astype(o_ref.dtype)


def _residual_epilogue(accs, extra_refs, o_ref, j, *, scale):
    o_ref[...] = extra_refs[0][...] + scale * accs[0]


def _gate_epilogue(accs, extra_refs, o_ref, j):
    g_a, g_b, u_a, u_b = accs
    o_ref[...] = (jax.nn.sigmoid(g_a) * u_a + jax.nn.sigmoid(g_b) * u_b).astype(o_ref.dtype)


def _rope_epilogue(accs, extra_refs, o_ref, j, *, tn, rope_flags):
    acc = accs[0]
    cos_ref, sin_ref = extra_refs
    assert len(rope_flags) < 31
    flag_bits = sum(int(f) << idx for idx, f in enumerate(rope_flags))
    is_rope = (lax.shift_right_logical(jnp.int32(flag_bits), j) & 1) == 1

    @pl.when(is_rope)
    def _():
        cos, sin = cos_ref[...], sin_ref[...]
        for h in range(tn // HEAD_DIM):
            x = acc[:, h * HEAD_DIM:(h + 1) * HEAD_DIM]
            o_ref[:, h * HEAD_DIM:(h + 1) * HEAD_DIM] = x * cos + pltpu.roll(x, HEAD_DIM // 2, 1) * sin

    @pl.when(jnp.logical_not(is_rope))
    def _():
        o_ref[...] = acc


def _ffn(x, h, w_in, w_out, layer, tm):
    tf = 256
    d_ff = w_out.shape[1]
    assert d_ff % tf == 0
    g = _matmul([h], [(w_in, layer, 0, 0), (w_in, layer, 0, d_ff // tf)], [], _swiglu_epilogue,
                d_ff, BF16, tm, tf, "ffn_in")
    tn, tm_out = 256, tm // 3
    return _matmul([g], [(w_out, layer, 0, 0)], [(x, (tm_out, tn), lambda j, i: (i, j))],
                   functools.partial(_residual_epilogue, scale=0.5),
                   x.shape[1], F32, tm_out, tn, "ffn_out")


def _band_kernel(*refs, n_kv, grp, max_dist, has_sink, want_lse):
    q_ref, kc_ref, kp_ref, vc_ref, vp_ref = refs[:5]
    pos = 5
    sink_ref = None
    if has_sink:
        sink_ref = refs[pos]
        pos += 1
    o_ref = refs[pos]
    lse_ref = refs[pos + 1] if want_lse else None

    first_key = jnp.where(pl.program_id(1) == 0, BLOCK, 0)
    rows = grp * BLOCK
    qi = lax.broadcasted_iota(jnp.int32, (rows, 2 * BLOCK), 0) & (BLOCK - 1)
    kj = lax.broadcasted_iota(jnp.int32, (rows, 2 * BLOCK), 1)
    dist = qi + BLOCK - kj
    valid = (dist >= 0) & (dist <= max_dist) & (kj >= first_key)
    scale = HEAD_DIM ** -0.5

    for kh in range(n_kv):
        ks = slice(kh * HEAD_DIM, (kh + 1) * HEAD_DIM)
        q = jnp.concatenate(
            [q_ref[:, (kh * grp + g) * HEAD_DIM:(kh * grp + g + 1) * HEAD_DIM] for g in range(grp)],
            axis=0).astype(BF16)
        k = jnp.concatenate([kp_ref[:, ks], kc_ref[:, ks]], axis=0).astype(BF16)
        v = jnp.concatenate([vp_ref[:, ks], vc_ref[:, ks]], axis=0).astype(BF16)
        s = jnp.where(valid, _dot_nt(q, k) * scale, NEG)
        m = jnp.max(s, axis=-1, keepdims=True)
        if has_sink:
            sink = jnp.concatenate(
                [jnp.full((BLOCK, 1), sink_ref[kh * grp + g], F32) for g in range(grp)], axis=0)
            m = jnp.maximum(m, sink)
        e = jnp.exp(s - m)
        l = jnp.sum(e, axis=-1, keepdims=True)
        denom = l + jnp.exp(sink - m) if has_sink else l
        o = _dot(e.astype(BF16), v) / denom
        for g in range(grp):
            hs = slice((kh * grp + g) * HEAD_DIM, (kh * grp + g + 1) * HEAD_DIM)
            o_ref[:, hs] = o[g * BLOCK:(g + 1) * BLOCK].astype(o_ref.dtype)
            if want_lse:
                lse = (m + jnp.log(l))[g * BLOCK:(g + 1) * BLOCK]
                lse_ref[:, hs] = jnp.broadcast_to(lse, (BLOCK, HEAD_DIM))


def _band_attention(qkv, n_batch, seq, dil, q_off, n_kv, grp, max_dist, sink, out_dtype, want_lse):
    tokens, width = qkv.shape
    wq, wk = n_kv * grp * HEAD_DIM, n_kv * HEAD_DIM
    k_off, v_off = q_off + wq, q_off + wq + wk
    assert q_off % wq == 0 and k_off % wk == 0 and v_off % wk == 0 and width % wq == 0
    sub = seq // dil
    assert seq % dil == 0 and sub % BLOCK == 0 and tokens % dil == 0
    nbl = sub // BLOCK
    view = qkv.reshape(tokens // dil, dil * width)

    def imap(off, w, prev):
        def f(s, b):
            blk = jnp.maximum(b - 1, 0) if prev else b
            return (s // dil) * nbl + blk, ((s % dil) * width + off) // w
        return f

    in_specs = [pl.BlockSpec((BLOCK, wq), imap(q_off, wq, False)),
                pl.BlockSpec((BLOCK, wk), imap(k_off, wk, False)),
                pl.BlockSpec((BLOCK, wk), imap(k_off, wk, True)),
                pl.BlockSpec((BLOCK, wk), imap(v_off, wk, False)),
                pl.BlockSpec((BLOCK, wk), imap(v_off, wk, True))]
    args = [view] * 5
    if sink is not None:
        in_specs.append(pl.BlockSpec(memory_space=pltpu.SMEM))
        args.append(sink)
    out_rows = n_batch * sub
    o_spec = pl.BlockSpec((BLOCK, wq), lambda s, b: ((s // dil) * nbl + b, s % dil))
    out_shape = [jax.ShapeDtypeStruct((out_rows, dil * wq), out_dtype)]
    out_specs = [o_spec]
    if want_lse:
        out_shape.append(jax.ShapeDtypeStruct((out_rows, dil * wq), F32))
        out_specs.append(o_spec)
    kernel = functools.partial(_band_kernel, n_kv=n_kv, grp=grp, max_dist=max_dist,
                               has_sink=sink is not None, want_lse=want_lse)
    outs = pl.pallas_call(
        kernel,
        grid=(n_batch * dil, nbl),
        in_specs=in_specs,
        out_specs=out_specs,
        out_shape=out_shape,
        compiler_params=_params(2),
        name=f"band_attn_d{dil}_g{grp}",
    )(*args)
    return [o.reshape(n_batch * seq, wq) for o in outs]


def _sample_kernel(*refs, n_kv, grp, dil, lb, max_dist, has_sink, want_lse, t_new):
    q_ref, kn_ref, vn_ref, c_ref = refs[:4]
    pos = 4
    sink_ref = None
    if has_sink:
        sink_ref = refs[pos]
        pos += 1
    o_ref = refs[pos]
    lse_ref = refs[pos + 1] if want_lse else None

    n_q = n_kv * grp
    wk = n_kv * HEAD_DIM
    rows = n_q * t_new
    n_cache = lb // dil
    n_keys = n_cache + BLOCK
    scale = HEAD_DIM ** -0.5

    def lane_block(x, idx, n):
        zero = jnp.zeros_like(x)
        return jnp.concatenate([x if i == idx else zero for i in range(n)], axis=1)

    q_bd = jnp.concatenate(
        [lane_block(q_ref[:, h * HEAD_DIM:(h + 1) * HEAD_DIM], h // grp, n_kv) for h in range(n_q)],
        axis=0).astype(BF16)
    pad = jnp.zeros((BLOCK - t_new, wk), F32)
    k_new = jnp.concatenate([kn_ref[...], pad], axis=0)
    v_new = jnp.concatenate([vn_ref[...], pad], axis=0)

    assert dil & (dil - 1) == 0 and t_new & (t_new - 1) == 0
    t_row = lax.broadcasted_iota(jnp.int32, (rows, n_keys), 0) & (t_new - 1)
    lane = lax.broadcasted_iota(jnp.int32, (rows, n_keys), 1)
    t_col = lax.broadcasted_iota(jnp.int32, (rows, 1), 0) & (t_new - 1)
    if has_sink:
        sink = jnp.concatenate([jnp.full((t_new, 1), sink_ref[h], F32) for h in range(n_q)], axis=0)

    o_all = jnp.zeros((rows, wk), F32)
    lse_all = jnp.zeros((rows, 1), F32)
    for r in range(min(dil, t_new)):
        base = r * 2 * wk
        k = jnp.concatenate([c_ref[:, base:base + wk], k_new], axis=0).astype(BF16)
        v = jnp.concatenate([c_ref[:, base + wk:base + 2 * wk], v_new], axis=0).astype(BF16)
        delta = jnp.where(lane < n_cache, lb + t_row - (r + dil * lane), t_row - (lane - n_cache))
        valid = ((delta >= 0) & (delta <= max_dist) & ((delta & (dil - 1)) == 0)
                 & (lane < n_cache + t_new))
        s = jnp.where(valid, _dot_nt(q_bd, k) * scale, NEG)
        m = jnp.max(s, axis=-1, keepdims=True)
        if has_sink:
            m = jnp.maximum(m, sink)
        e = jnp.exp(s - m)
        l = jnp.sum(e, axis=-1, keepdims=True)
        denom = l + jnp.exp(sink - m) if has_sink else l
        o = _dot(e.astype(BF16), v) / denom
        mine = ((t_col + lb) & (dil - 1)) == r
        o_all = jnp.where(mine, o, o_all)
        lse_all = jnp.where(mine, m + jnp.log(l), lse_all)

    for h in range(n_q):
        hs = slice(h * HEAD_DIM, (h + 1) * HEAD_DIM)
        ks = slice((h // grp) * HEAD_DIM, (h // grp + 1) * HEAD_DIM)
        o_ref[:, hs] = o_all[h * t_new:(h + 1) * t_new, ks].astype(o_ref.dtype)
        if want_lse:
            lse_ref[:, hs] = jnp.broadcast_to(lse_all[h * t_new:(h + 1) * t_new], (t_new, HEAD_DIM))


def _sample_attention(qkv, cache, layer, row0, n_req, t_new, dil, q_off, n_kv, grp, max_dist, sink,
                      out_dtype, want_lse):
    depth, n_req_c, lb, two, n_kv_c, hd = cache.shape
    assert (n_req_c, two, n_kv_c, hd) == (n_req, 2, n_kv, HEAD_DIM)
    assert lb % dil == 0 and (lb // dil) % BLOCK == 0 and row0 % t_new == 0 and t_new % 8 == 0
    wq, wk = n_kv * grp * HEAD_DIM, n_kv * HEAD_DIM
    k_off, v_off = q_off + wq, q_off + wq + wk
    assert q_off % wq == 0 and k_off % wk == 0 and v_off % wk == 0
    n_cls = min(dil, t_new)
    cview = cache.reshape(depth, n_req, lb // dil, dil * 2 * wk)
    rb0 = row0 // t_new
    in_specs = [pl.BlockSpec((t_new, wq), lambda b: (rb0 + b, q_off // wq)),
                pl.BlockSpec((t_new, wk), lambda b: (rb0 + b, k_off // wk)),
                pl.BlockSpec((t_new, wk), lambda b: (rb0 + b, v_off // wk)),
                pl.BlockSpec((None, None, lb // dil, n_cls * 2 * wk), lambda b: (layer, b, 0, 0))]
    args = [qkv, qkv, qkv, cview]
    if sink is not None:
        in_specs.append(pl.BlockSpec(memory_space=pltpu.SMEM))
        args.append(sink)
    o_spec = pl.BlockSpec((t_new, wq), lambda b: (b, 0))
    out_shape = [jax.ShapeDtypeStruct((n_req * t_new, wq), out_dtype)]
    out_specs = [o_spec]
    if want_lse:
        out_shape.append(jax.ShapeDtypeStruct((n_req * t_new, wq), F32))
        out_specs.append(o_spec)
    kernel = functools.partial(_sample_kernel, n_kv=n_kv, grp=grp, dil=dil, lb=lb, max_dist=max_dist,
                               has_sink=sink is not None, want_lse=want_lse, t_new=t_new)
    return pl.pallas_call(
        kernel,
        grid=(n_req,),
        in_specs=in_specs,
        out_specs=out_specs,
        out_shape=out_shape,
        compiler_params=_params(1),
        name=f"sample_attn_d{dil}_g{grp}",
    )(*args)


def _combine_kernel(o1, o2, o3, l1, l2, l3, out_ref):
    a, b, c = l1[...], l2[...], l3[...]
    m = jnp.maximum(jnp.maximum(a, b), c)
    wa, wb, wc = jnp.exp(a - m), jnp.exp(b - m), jnp.exp(c - m)
    tot = wa + wb + wc
    out_ref[...] = ((wa / tot) * o1[...] + (wb / tot) * o2[...] + (wc / tot) * o3[...]).astype(out_ref.dtype)


def _combine(outs, lses, tm):
    m, w = outs[0].shape
    spec = pl.BlockSpec((tm, w), lambda i: (i, 0))
    return pl.pallas_call(
        _combine_kernel,
        grid=(m // tm,),
        in_specs=[spec] * 6,
        out_specs=spec,
        out_shape=jax.ShapeDtypeStruct((m, w), BF16),
        compiler_params=_params(1),
        name="combine",
    )(*outs, *lses)


def _rope_tables(positions):
    half = HEAD_DIM // 2
    inv = ROPE_THETA ** (-jnp.arange(half, dtype=F32) / half)
    ang = positions.astype(F32)[:, None] * inv[None, :]
    cos, sin = jnp.cos(ang), jnp.sin(ang)
    return jnp.concatenate([cos, cos], axis=1), jnp.concatenate([-sin, sin], axis=1)


def kernel(x_prompt, x_sample, cache_a_kv, cache_b1_kv, cache_b2_kv, cache_b3_kv, norm_ffn1, w_ffn1_in,
           w_ffn1_out, norm_mix, w_in, sinks, w_gate, w_up_a, w_up_b, w_o, norm_ffn2, w_ffn2_in,
           w_ffn2_out, norm_final):
    batch, seq, d_model = x_prompt.shape
    n_req, t_new, _ = x_sample.shape
    depth = w_in.shape[0]
    n_heads_a = sinks.shape[1]
    n_kv_a = cache_a_kv.shape[4]
    grp_a = n_heads_a // n_kv_a
    n_heads_b = cache_b1_kv.shape[4]
    w_qa, w_ka, w_b = n_heads_a * HEAD_DIM, n_kv_a * HEAD_DIM, n_heads_b * HEAD_DIM
    in_width = w_in.shape[2]
    assert in_width == w_qa + 2 * w_ka + 9 * w_b
    caches_b = (cache_b1_kv, cache_b2_kv, cache_b3_kv)

    n_p, n_s = batch * seq, n_req * t_new
    n_tok = n_p + n_s
    tm = n_tok // 8
    assert n_tok % (8 * 48) == 0

    x = jnp.concatenate([x_prompt.reshape(n_p, d_model), x_sample.reshape(n_s, d_model)], axis=0)
    positions = jnp.concatenate([jnp.tile(jnp.arange(seq), batch),
                                 jnp.tile(PAST_LEN + jnp.arange(t_new), n_req)])
    cos_t, sin_t = _rope_tables(positions)

    tn_qkv = 512
    seg = [(w_qa, True), (w_ka, True), (w_ka, False)] + [(w_b, True), (w_b, True), (w_b, False)] * 3
    rope_flags = tuple(int(f) for w, f in seg for _ in range(w // tn_qkv))
    assert len(rope_flags) == in_width // tn_qkv
    b_off = [w_qa + 2 * w_ka + 3 * w_b * g for g in range(3)]

    new_a_p, new_a_s = [], []
    new_b_p = [[] for _ in DILATED_PAIRS]
    new_b_s = [[] for _ in DILATED_PAIRS]
    tm_norm = n_tok // 22
    for l in range(depth):
        h = _rmsnorm(x, norm_ffn1[l], BF16, tm_norm)
        x = _ffn(x, h, w_ffn1_in, w_ffn1_out, l, tm)

        h = _rmsnorm(x, norm_mix[l], BF16, tm_norm)
        qkv = _matmul([h], [(w_in, l, 0, 0)],
                      [(cos_t, (tm, HEAD_DIM), lambda j, i: (i, 0)), (sin_t, (tm, HEAD_DIM), lambda j, i: (i, 0))],
                      functools.partial(_rope_epilogue, tn=tn_qkv, rope_flags=rope_flags),
                      in_width, F32, tm, tn_qkv, "qkv_rope")

        (oa_p,) = _band_attention(qkv, batch, seq, 1, 0, n_kv_a, grp_a, WINDOW_A - 1, sinks[l], BF16, False)
        (oa_s,) = _sample_attention(qkv, cache_a_kv, l, n_p, n_req, t_new, 1, 0, n_kv_a, grp_a,
                                    WINDOW_A - 1, sinks[l], BF16, False)
        outs_p, lses_p, outs_s, lses_s = [], [], [], []
        for g, (window, dil) in enumerate(DILATED_PAIRS):
            o, lse = _band_attention(qkv, batch, seq, dil, b_off[g], n_heads_b, 1, window // dil, None, F32, True)
            outs_p.append(o)
            lses_p.append(lse)
            o, lse = _sample_attention(qkv, caches_b[g], l, n_p, n_req, t_new, dil, b_off[g], n_heads_b, 1,
                                       window, None, F32, True)
            outs_s.append(o)
            lses_s.append(lse)
        o_a = jnp.concatenate([oa_p, oa_s], axis=0)
        o_b = jnp.concatenate([_combine(outs_p, lses_p, 512), _combine(outs_s, lses_s, n_s)], axis=0)

        tn_g = 256
        u = _matmul([h, o_a, o_b],
                    [(w_gate, l, 0, 0), (w_gate, l, 0, d_model // tn_g), (w_up_a, l, 1, 0), (w_up_b, l, 2, 0)],
                    [], _gate_epilogue, d_model, BF16, tm // 2, tn_g, "gate_up")
        x = _matmul([u], [(w_o, l, 0, 0)], [(x, (tm, 512), lambda j, i: (i, j))],
                    functools.partial(_residual_epilogue, scale=1.0), d_model, F32, tm, 512, "w_o")

        h = _rmsnorm(x, norm_ffn2[l], BF16, tm_norm)
        x = _ffn(x, h, w_ffn2_in, w_ffn2_out, l, tm)

        qkv_p = qkv[:n_p].reshape(batch, seq, in_width)
        qkv_s = qkv[n_p:].reshape(n_req, t_new, in_width)
        lw = min(WINDOW_A, seq)
        new_a_p.append(qkv_p[:, seq - lw:, w_qa:w_qa + 2 * w_ka].reshape(batch, lw, 2, n_kv_a, HEAD_DIM))
        new_a_s.append(qkv_s[:, :, w_qa:w_qa + 2 * w_ka].reshape(n_req, t_new, 2, n_kv_a, HEAD_DIM))
        for g, (window, dil) in enumerate(DILATED_PAIRS):
            lw = min(window, seq)
            c0 = b_off[g] + w_b
            new_b_p[g].append(qkv_p[:, seq - lw:, c0:c0 + 2 * w_b].reshape(batch, lw, 2, n_heads_b, HEAD_DIM))
            new_b_s[g].append(qkv_s[:, :, c0:c0 + 2 * w_b].reshape(n_req, t_new, 2, n_heads_b, HEAD_DIM))

    y = _rmsnorm(x, norm_final, F32, tm_norm)
    y_prompt = y[:n_p].reshape(batch, seq, d_model)
    y_sample = y[n_p:].reshape(n_req, t_new, d_model)
    return (y_prompt, y_sample, jnp.stack(new_a_p), jnp.stack(new_a_s),
            jnp.stack(new_b_p[0]), jnp.stack(new_b_s[0]), jnp.stack(new_b_p[1]), jnp.stack(new_b_s[1]),
            jnp.stack(new_b_p[2]), jnp.stack(new_b_s[2]))
```

```python
import functools

import jax
import jax.numpy as jnp
from jax import lax
from jax.experimental import pallas as pl
from jax.experimental.pallas import tpu as pltpu

HEAD_DIM = 128
BLOCK = 128
WINDOW_A = 128
PAST_LEN = 16384
DILATED_PAIRS = ((128, 1), (512, 4), (2048, 16))
ROPE_THETA = 10000.0
EPS = 1e-5
NEG = -1e30
VMEM_LIMIT = 56 * 1024 * 1024

F32 = jnp.float32
BF16 = jnp.bfloat16


def _params(n_axes, vmem=VMEM_LIMIT):
    return pltpu.CompilerParams(dimension_semantics=("arbitrary",) * n_axes, vmem_limit_bytes=vmem)


def _dot(a, b):
    return jnp.dot(a, b, preferred_element_type=F32)


def _dot_nt(a, b):
    return lax.dot_general(a, b, (((1,), (1,)), ((), ())), preferred_element_type=F32)


def _head(h):
    return slice(h * HEAD_DIM, (h + 1) * HEAD_DIM)


def _rmsnorm_kernel(x_ref, g_ref, o_ref):
    x = x_ref[...]
    y = x * lax.rsqrt(jnp.mean(x * x, axis=-1, keepdims=True) + EPS)
    o_ref[...] = (y * g_ref[...]).astype(o_ref.dtype)


def _rmsnorm(x, g, out_dtype, tm, row0=0, rows=None):
    d = x.shape[1]
    rows = x.shape[0] if rows is None else rows
    assert rows % tm == 0 and row0 % tm == 0
    blk0 = row0 // tm
    return pl.pallas_call(
        _rmsnorm_kernel,
        grid=(rows // tm,),
        in_specs=[pl.BlockSpec((tm, d), lambda i: (blk0 + i, 0)), pl.BlockSpec((1, d), lambda i: (0, 0))],
        out_specs=pl.BlockSpec((tm, d), lambda i: (i, 0)),
        out_shape=jax.ShapeDtypeStruct((rows, d), out_dtype),
        compiler_params=_params(1),
        name="rmsnorm",
    )(x, g.reshape(1, d))


def _mm_kernel(*refs, n_a, b_to_a, n_extra, epilogue):
    n_b = len(b_to_a)
    a_refs = refs[:n_a]
    b_refs = refs[n_a:n_a + n_b]
    extra_refs = refs[n_a + n_b:n_a + n_b + n_extra]
    o_ref = refs[n_a + n_b + n_extra]
    w_refs = refs[n_a + n_b + n_extra + 1:]

    @pl.when(pl.program_id(1) == 0)
    def _():
        for b_ref, w_ref in zip(b_refs, w_refs):
            w_ref[...] = b_ref[...].astype(BF16)

    accs = [_dot(a_refs[ai][...], w_refs[k][...]) for k, ai in enumerate(b_to_a)]
    epilogue(accs, extra_refs, o_ref, pl.program_id(0))


def _matmul(a_list, b_list, extras, epilogue, n_out, out_dtype, tm, tn, name):
    m = a_list[0].shape[0]
    assert m % tm == 0 and n_out % tn == 0
    in_specs = [pl.BlockSpec((tm, a.shape[1]), lambda j, i: (i, 0)) for a in a_list]
    for w, layer, ai, off in b_list:
        assert w.shape[1] == a_list[ai].shape[1] and w.shape[2] % tn == 0
        in_specs.append(pl.BlockSpec((None, w.shape[1], tn), lambda j, i, layer=layer, off=off: (layer, 0, j + off)))
    for arr, shape, imap in extras:
        in_specs.append(pl.BlockSpec(shape, imap))
    kernel = functools.partial(_mm_kernel, n_a=len(a_list), b_to_a=tuple(b[2] for b in b_list),
                               n_extra=len(extras), epilogue=epilogue)
    return pl.pallas_call(
        kernel,
        grid=(n_out // tn, m // tm),
        in_specs=in_specs,
        out_specs=pl.BlockSpec((tm, tn), lambda j, i: (i, j)),
        out_shape=jax.ShapeDtypeStruct((m, n_out), out_dtype),
        scratch_shapes=[pltpu.VMEM((b[0].shape[1], tn), BF16) for b in b_list],
        compiler_params=_params(2),
        name=name,
    )(*a_list, *[b[0] for b in b_list], *[arr for arr, _, _ in extras])


def _mm_ksplit_kernel(a_ref, b_ref, x_ref, o_ref, w_ref, acc_ref, *, n_k, tm, scale):
    kk, i = pl.program_id(1), pl.program_id(2)

    @pl.when(i == 0)
    def _():
        w_ref[...] = b_ref[...].astype(BF16)

    part = _dot(a_ref[...], w_ref[...])
    rows = pl.ds(pl.multiple_of(i * tm, tm), tm)

    @pl.when(kk == 0)
    def _():
        acc_ref[rows, :] = part

    if n_k > 2:
        @pl.when((kk > 0) & (kk < n_k - 1))
        def _():
            acc_ref[rows, :] += part

    @pl.when(kk == n_k - 1)
    def _():
        o_ref[...] = x_ref[...] + scale * (acc_ref[rows, :] + part)


def _matmul_ksplit_residual(a, w, layer, x, scale, n_k, tm, tn, name):
    m, k = a.shape
    n_out = w.shape[2]
    assert w.shape[1] == k and k % n_k == 0 and m % tm == 0 and n_out % tn == 0 and n_k >= 2
    tk = k // n_k
    last = n_k - 1
    xo_map = lambda j, kk, i: (jnp.where(kk == last, i, 0), j)
    return pl.pallas_call(
        functools.partial(_mm_ksplit_kernel, n_k=n_k, tm=tm, scale=scale),
        grid=(n_out // tn, n_k, m // tm),
        in_specs=[pl.BlockSpec((tm, tk), lambda j, kk, i: (i, kk)),
                  pl.BlockSpec((None, tk, tn), lambda j, kk, i: (layer, kk, j)),
                  pl.BlockSpec((tm, tn), xo_map)],
        out_specs=pl.BlockSpec((tm, tn), xo_map),
        out_shape=jax.ShapeDtypeStruct((m, n_out), F32),
        scratch_shapes=[pltpu.VMEM((tk, tn), BF16), pltpu.VMEM((m, tn), F32)],
        compiler_params=_params(3),
        name=name,
    )(a, w, x)


def _swiglu_epilogue(accs, extra_refs, o_ref, j):
    a, b = accs
    o_ref[...] = (a * jax.nn.sigmoid(a) * b).astype(o_ref.dtype)


def _residual_epilogue(accs, extra_refs, o_ref, j, *, scale):
    o_ref[...] = extra_refs[0][...] + scale * accs[0]


def _gate_epilogue(accs, extra_refs, o_ref, j):
    g_a, g_b, u_a, u_b = accs
    o_ref[...] = (jax.nn.sigmoid(g_a) * u_a + jax.nn.sigmoid(g_b) * u_b).astype(o_ref.dtype)


def _rope_epilogue(accs, extra_refs, o_ref, j, *, tn, rope_flags):
    acc = accs[0]
    cos_ref, sin_ref = extra_refs
    assert len(rope_flags) < 31
    flag_bits = sum(int(f) << idx for idx, f in enumerate(rope_flags))
    is_rope = (lax.shift_right_logical(jnp.int32(flag_bits), j) & 1) == 1

    @pl.when(is_rope)
    def _():
        cos, sin = cos_ref[...], sin_ref[...]
        for h in range(tn // HEAD_DIM):
            x = acc[:, _head(h)]
            o_ref[:, _head(h)] = x * cos + pltpu.roll(x, HEAD_DIM // 2, 1) * sin

    @pl.when(jnp.logical_not(is_rope))
    def _():
        o_ref[...] = acc


def _ffn(x, h, w_in, w_out, layer, tm):
    tf = 256
    d_ff = w_out.shape[1]
    assert d_ff % tf == 0
    g = _matmul([h], [(w_in, layer, 0, 0), (w_in, layer, 0, d_ff // tf)], [], _swiglu_epilogue,
                d_ff, BF16, tm, tf, "ffn_in")
    return _matmul_ksplit_residual(g, w_out, layer, x, 0.5, 2, tm, 256, "ffn_out")


def _softmax_pv(s, valid, v, sink=None):
    s = jnp.where(valid, s, NEG)
    m = jnp.max(s, axis=-1, keepdims=True)
    if sink is not None:
        m = jnp.maximum(m, sink)
    e = jnp.exp(s - m)
    l = jnp.sum(e, axis=-1, keepdims=True)
    denom = l if sink is None else l + jnp.exp(sink - m)
    return _dot(e.astype(BF16), v) / denom, m + jnp.log(l)


def _swa_prompt_kernel(q_ref, kc_ref, kp_ref, vc_ref, vp_ref, sink_ref, o_ref, *, n_kv, grp, max_dist):
    first_key = jnp.where(pl.program_id(1) == 0, BLOCK, 0)
    rows = grp * BLOCK
    qi = lax.broadcasted_iota(jnp.int32, (rows, 2 * BLOCK), 0) & (BLOCK - 1)
    kj = lax.broadcasted_iota(jnp.int32, (rows, 2 * BLOCK), 1)
    dist = qi + BLOCK - kj
    valid = (dist >= 0) & (dist <= max_dist) & (kj >= first_key)
    scale = HEAD_DIM ** -0.5
    for kh in range(n_kv):
        q = jnp.concatenate([q_ref[:, _head(kh * grp + g)] for g in range(grp)], axis=0).astype(BF16)
        k = jnp.concatenate([kp_ref[:, _head(kh)], kc_ref[:, _head(kh)]], axis=0).astype(BF16)
        v = jnp.concatenate([vp_ref[:, _head(kh)], vc_ref[:, _head(kh)]], axis=0).astype(BF16)
        sink = jnp.concatenate(
            [jnp.full((BLOCK, 1), sink_ref[kh * grp + g], F32) for g in range(grp)], axis=0)
        o, _ = _softmax_pv(_dot_nt(q, k) * scale, valid, v, sink)
        for g in range(grp):
            o_ref[:, _head(kh * grp + g)] = o[g * BLOCK:(g + 1) * BLOCK].astype(o_ref.dtype)


def _swa_prompt(qkv, n_batch, seq, n_kv, grp, max_dist, sink):
    wq, wk = n_kv * grp * HEAD_DIM, n_kv * HEAD_DIM
    assert seq % BLOCK == 0 and wq % wk == 0
    nbl = seq // BLOCK
    k_blk, v_blk = wq // wk, wq // wk + 1

    def imap(col, prev):
        return lambda n, b: (n * nbl + (jnp.maximum(b - 1, 0) if prev else b), col)

    return pl.pallas_call(
        functools.partial(_swa_prompt_kernel, n_kv=n_kv, grp=grp, max_dist=max_dist),
        grid=(n_batch, nbl),
        in_specs=[pl.BlockSpec((BLOCK, wq), imap(0, False)),
                  pl.BlockSpec((BLOCK, wk), imap(k_blk, False)), pl.BlockSpec((BLOCK, wk), imap(k_blk, True)),
                  pl.BlockSpec((BLOCK, wk), imap(v_blk, False)), pl.BlockSpec((BLOCK, wk), imap(v_blk, True)),
                  pl.BlockSpec(memory_space=pltpu.SMEM)],
        out_specs=pl.BlockSpec((BLOCK, wq), lambda n, b: (n * nbl + b, 0)),
        out_shape=jax.ShapeDtypeStruct((n_batch * seq, wq), BF16),
        compiler_params=_params(2),
        name="swa_prompt",
    )(qkv, qkv, qkv, qkv, qkv, sink)


def _dilated_prompt_kernel(*refs, groups, seq):
    n_g = len(groups)
    qkv_refs = refs[:3 * n_g]
    o_ref = refs[3 * n_g]
    o_scr = refs[3 * n_g + 1:3 * n_g + 1 + n_g]
    l_scr = refs[3 * n_g + 1 + n_g:]
    scale = HEAD_DIM ** -0.5

    def band(n_keys, off, max_dist):
        qi = lax.broadcasted_iota(jnp.int32, (BLOCK, n_keys), 0)
        kj = lax.broadcasted_iota(jnp.int32, (BLOCK, n_keys), 1)
        dist = qi + off - kj
        return (dist >= 0) & (dist <= max_dist)

    for g, (max_dist, dil) in enumerate(groups):
        q_ref, k_ref, v_ref = qkv_refs[3 * g:3 * g + 3]
        valid_first, valid_next = band(BLOCK, 0, max_dist), band(2 * BLOCK, BLOCK, max_dist)

        def rows(start, dil=dil):
            return pl.ds(start, BLOCK, stride=dil) if dil > 1 else pl.ds(start, BLOCK)

        for r in range(dil):
            for b in range(seq // dil // BLOCK):
                cur = rows(r + b * BLOCK * dil)
                q = q_ref[cur, :].astype(BF16)
                if b == 0:
                    k, v, valid = k_ref[cur, :], v_ref[cur, :], valid_first
                else:
                    prev = rows(r + (b - 1) * BLOCK * dil)
                    k = jnp.concatenate([k_ref[prev, :], k_ref[cur, :]], axis=0)
                    v = jnp.concatenate([v_ref[prev, :], v_ref[cur, :]], axis=0)
                    valid = valid_next
                o, lse = _softmax_pv(_dot_nt(q, k.astype(BF16)) * scale, valid, v.astype(BF16))
                o_scr[g][cur, :] = o
                l_scr[g][cur, :] = jnp.broadcast_to(lse, (BLOCK, HEAD_DIM))

    chunk = 2 * BLOCK
    for c in range(seq // chunk):
        cs = pl.ds(c * chunk, chunk)
        lses = [l[cs, :] for l in l_scr]
        m = functools.reduce(jnp.maximum, lses)
        ws = [jnp.exp(l - m) for l in lses]
        tot = functools.reduce(jnp.add, ws)
        mix = functools.reduce(jnp.add, [(w / tot) * o[cs, :] for w, o in zip(ws, o_scr)])
        o_ref[cs, :] = mix.astype(o_ref.dtype)


def _dilated_prompt(qkv, n_batch, seq, col0, n_heads, pairs):
    groups = tuple((window // dil, dil) for window, dil in pairs)
    assert all(seq % (dil * BLOCK) == 0 and window // dil <= BLOCK for window, dil in pairs)
    w = n_heads * HEAD_DIM
    in_specs = []
    for c in col0:
        assert c % HEAD_DIM == 0
        for part in range(3):
            cb = (c + part * w) // HEAD_DIM
            in_specs.append(pl.BlockSpec((seq, HEAD_DIM), lambda n, h, cb=cb: (n, cb + h)))
    n_g = len(groups)
    return pl.pallas_call(
        functools.partial(_dilated_prompt_kernel, groups=groups, seq=seq),
        grid=(n_batch, n_heads),
        in_specs=in_specs,
        out_specs=pl.BlockSpec((seq, HEAD_DIM), lambda n, h: (n, h)),
        out_shape=jax.ShapeDtypeStruct((n_batch * seq, w), BF16),
        scratch_shapes=[pltpu.VMEM((seq, HEAD_DIM), F32) for _ in range(2 * n_g)],
        compiler_params=_params(2),
        name="dilated_prompt",
    )(*([qkv] * (3 * n_g)))


def _export_kernel(k_ref, v_ref, prev_ref, o_ref, *, n_kv):
    del prev_ref
    for h in range(n_kv):
        o_ref[:, 0, h, :] = k_ref[:, _head(h)]
        o_ref[:, 1, h, :] = v_ref[:, _head(h)]


def _export_window(qkv, prev, layer, depth, n_batch, seq, lw, k_col, n_kv):
    wk = n_kv * HEAD_DIM
    rb = min(lw, 256)
    assert lw % rb == 0 and seq % rb == 0 and k_col % wk == 0
    shape = (depth, n_batch, lw, 2, n_kv, HEAD_DIM)
    first = prev is None
    if first:
        prev = jnp.zeros((8, HEAD_DIM), F32)
    row_blk0 = (seq - lw) // rb

    def imap(part):
        return lambda n, c: (n * (seq // rb) + row_blk0 + c, k_col // wk + part)

    return pl.pallas_call(
        functools.partial(_export_kernel, n_kv=n_kv),
        grid=(n_batch, lw // rb),
        in_specs=[pl.BlockSpec((rb, wk), imap(0)), pl.BlockSpec((rb, wk), imap(1)),
                  pl.BlockSpec(memory_space=pl.ANY)],
        out_specs=pl.BlockSpec((None, None, rb, 2, n_kv, HEAD_DIM), lambda n, c: (layer, n, c, 0, 0, 0)),
        out_shape=jax.ShapeDtypeStruct(shape, F32),
        input_output_aliases={} if first else {2: 0},
        compiler_params=_params(2),
        name="export_window",
    )(qkv, qkv, prev)


def _sample_group(q_ref, q_col, new_ref, k_col, c_ref, *, n_kv, grp, dil, lb, max_dist, sink, t_new):
    n_q = n_kv * grp
    wk = n_kv * HEAD_DIM
    rows = n_q * t_new
    n_cache = lb // dil
    n_keys = n_cache + BLOCK
    scale = HEAD_DIM ** -0.5
    assert dil & (dil - 1) == 0 and t_new & (t_new - 1) == 0

    def lane_block(x, idx, n):
        zero = jnp.zeros_like(x)
        return jnp.concatenate([x if i == idx else zero for i in range(n)], axis=1)

    q_bd = jnp.concatenate(
        [lane_block(q_ref[:, q_col + h * HEAD_DIM:q_col + (h + 1) * HEAD_DIM], h // grp, n_kv)
         for h in range(n_q)], axis=0).astype(BF16)
    pad = jnp.zeros((BLOCK - t_new, wk), F32)
    k_new = jnp.concatenate([new_ref[:, k_col:k_col + wk], pad], axis=0)
    v_new = jnp.concatenate([new_ref[:, k_col + wk:k_col + 2 * wk], pad], axis=0)

    t_row = lax.broadcasted_iota(jnp.int32, (rows, n_keys), 0) & (t_new - 1)
    lane = lax.broadcasted_iota(jnp.int32, (rows, n_keys), 1)
    t_col = lax.broadcasted_iota(jnp.int32, (rows, 1), 0) & (t_new - 1)

    o_all = jnp.zeros((rows, wk), F32)
    lse_all = jnp.zeros((rows, 1), F32)
    for r in range(min(dil, t_new)):
        k_c = jnp.concatenate([c_ref[:, 2 * r, h, :] for h in range(n_kv)], axis=1)
        v_c = jnp.concatenate([c_ref[:, 2 * r + 1, h, :] for h in range(n_kv)], axis=1)
        k = jnp.concatenate([k_c, k_new], axis=0).astype(BF16)
        v = jnp.concatenate([v_c, v_new], axis=0).astype(BF16)
        delta = jnp.where(lane < n_cache, lb + t_row - (r + dil * lane), t_row - (lane - n_cache))
        valid = ((delta >= 0) & (delta <= max_dist) & ((delta & (dil - 1)) == 0)
                 & (lane < n_cache + t_new))
        o, lse = _softmax_pv(_dot_nt(q_bd, k) * scale, valid, v, sink)
        mine = ((t_col + lb) & (dil - 1)) == r
        o_all = jnp.where(mine, o, o_all)
        lse_all = jnp.where(mine, lse, lse_all)
    return o_all, lse_all


def _sample_kernel(*refs, cfg):
    t_new, n_kv_a, grp_a, n_heads_b, pairs, lbs, lb_a, col_b = (
        cfg["t_new"], cfg["n_kv_a"], cfg["grp_a"], cfg["n_heads_b"], cfg["pairs"], cfg["lbs"],
        cfg["lb_a"], cfg["col_b"])
    n_g = len(pairs)
    qkv_ref, ca_ref = refs[0], refs[1]
    cb_refs = refs[2:2 + n_g]
    sink_ref = refs[2 + n_g]
    pos = 3 + n_g + (1 + n_g)
    oa_ref, ob_ref, na_ref = refs[pos], refs[pos + 1], refs[pos + 2]
    nb_refs = refs[pos + 3:pos + 3 + n_g]

    n_q_a = n_kv_a * grp_a
    w_qa, w_ka, w_b = n_q_a * HEAD_DIM, n_kv_a * HEAD_DIM, n_heads_b * HEAD_DIM

    sink = jnp.concatenate([jnp.full((t_new, 1), sink_ref[h], F32) for h in range(n_q_a)], axis=0)
    o, _ = _sample_group(qkv_ref, 0, qkv_ref, w_qa, ca_ref, n_kv=n_kv_a, grp=grp_a, dil=1, lb=lb_a,
                         max_dist=WINDOW_A - 1, sink=sink, t_new=t_new)
    for h in range(n_q_a):
        oa_ref[:, _head(h)] = o[h * t_new:(h + 1) * t_new, _head(h // grp_a)].astype(oa_ref.dtype)
    for h in range(n_kv_a):
        na_ref[:, 0, h, :] = qkv_ref[:, w_qa + h * HEAD_DIM:w_qa + (h + 1) * HEAD_DIM]
        na_ref[:, 1, h, :] = qkv_ref[:, w_qa + w_ka + h * HEAD_DIM:w_qa + w_ka + (h + 1) * HEAD_DIM]

    outs, lses = [], []
    for g, (window, dil) in enumerate(pairs):
        c0 = col_b[g]
        o, lse = _sample_group(qkv_ref, c0, qkv_ref, c0 + w_b, cb_refs[g], n_kv=n_heads_b, grp=1, dil=dil,
                               lb=lbs[g], max_dist=window, sink=None, t_new=t_new)
        outs.append(o)
        lses.append(lse)
        for h in range(n_heads_b):
            nb_refs[g][:, 0, h, :] = qkv_ref[:, c0 + w_b + h * HEAD_DIM:c0 + w_b + (h + 1) * HEAD_DIM]
            nb_refs[g][:, 1, h, :] = qkv_ref[:, c0 + 2 * w_b + h * HEAD_DIM:c0 + 2 * w_b + (h + 1) * HEAD_DIM]
    m = functools.reduce(jnp.maximum, lses)
    ws = [jnp.exp(l - m) for l in lses]
    tot = functools.reduce(jnp.add, ws)
    mix = functools.reduce(jnp.add, [(w / tot) * o for w, o in zip(ws, outs)])
    for h in range(n_heads_b):
        ob_ref[:, _head(h)] = mix[h * t_new:(h + 1) * t_new, _head(h)].astype(ob_ref.dtype)


def _sample_step(qkv, row0, cache_a, caches_b, sink, prev_new, layer, pairs, col_b, t_new):
    depth, n_req, lb_a, _, n_kv_a, _ = cache_a.shape
    n_q_a = sink.shape[0]
    n_heads_b = caches_b[0].shape[4]
    in_width = qkv.shape[1]
    assert row0 % t_new == 0 and t_new % 8 == 0
    lbs = tuple(c.shape[2] for c in caches_b)
    cfg = dict(t_new=t_new, n_kv_a=n_kv_a, grp_a=n_q_a // n_kv_a, n_heads_b=n_heads_b, pairs=pairs,
               lbs=lbs, lb_a=lb_a, col_b=col_b)

    def cache_spec(cache, lb, dil):
        n_kv = cache.shape[4]
        assert lb % dil == 0 and lb // dil == BLOCK
        view = cache.reshape(depth, n_req, lb // dil, dil * 2, n_kv, HEAD_DIM)
        n_cls = min(dil, t_new)
        return view, pl.BlockSpec((None, None, lb // dil, n_cls * 2, n_kv, HEAD_DIM),
                                  lambda b: (layer, b, 0, 0, 0, 0))

    args = [qkv]
    in_specs = [pl.BlockSpec((t_new, in_width), lambda b: (row0 // t_new + b, 0))]
    view, spec = cache_spec(cache_a, lb_a, 1)
    args.append(view)
    in_specs.append(spec)
    for cache, lb, (window, dil) in zip(caches_b, lbs, pairs):
        view, spec = cache_spec(cache, lb, dil)
        args.append(view)
        in_specs.append(spec)
    args.append(sink)
    in_specs.append(pl.BlockSpec(memory_space=pltpu.SMEM))

    new_shapes = [(depth, n_req, t_new, 2, n_kv_a, HEAD_DIM)] + [(depth, n_req, t_new, 2, n_heads_b, HEAD_DIM)] * len(pairs)
    first = prev_new is None
    if first:
        prev_new = [jnp.zeros((8, HEAD_DIM), F32) for _ in new_shapes]
    n_in = len(args)
    args += list(prev_new)
    in_specs += [pl.BlockSpec(memory_space=pl.ANY)] * len(prev_new)

    n_s = n_req * t_new
    out_shape = [jax.ShapeDtypeStruct((n_s, n_q_a * HEAD_DIM), BF16),
                 jax.ShapeDtypeStruct((n_s, n_heads_b * HEAD_DIM), BF16)]
    out_specs = [pl.BlockSpec((t_new, n_q_a * HEAD_DIM), lambda b: (b, 0)),
                 pl.BlockSpec((t_new, n_heads_b * HEAD_DIM), lambda b: (b, 0))]
    for s in new_shapes:
        out_shape.append(jax.ShapeDtypeStruct(s, F32))
        out_specs.append(pl.BlockSpec((None, None) + s[2:], lambda b: (layer, b, 0, 0, 0, 0)))
    aliases = {} if first else {n_in + i: 2 + i for i in range(len(new_shapes))}
    outs = pl.pallas_call(
        functools.partial(_sample_kernel, cfg=cfg),
        grid=(n_req,),
        in_specs=in_specs,
        out_specs=out_specs,
        out_shape=out_shape,
        input_output_aliases=aliases,
        compiler_params=_params(1),
        name="sample_step",
    )(*args)
    return outs[0], outs[1], list(outs[2:])


def _rope_tables(positions):
    half = HEAD_DIM // 2
    inv = ROPE_THETA ** (-jnp.arange(half, dtype=F32) / half)
    ang = positions.astype(F32)[:, None] * inv[None, :]
    cos, sin = jnp.cos(ang), jnp.sin(ang)
    return jnp.concatenate([cos, cos], axis=1), jnp.concatenate([-sin, sin], axis=1)


def kernel(x_prompt, x_sample, cache_a_kv, cache_b1_kv, cache_b2_kv, cache_b3_kv, norm_ffn1, w_ffn1_in,
           w_ffn1_out, norm_mix, w_in, sinks, w_gate, w_up_a, w_up_b, w_o, norm_ffn2, w_ffn2_in,
           w_ffn2_out, norm_final):
    batch, seq, d_model = x_prompt.shape
    n_req, t_new, _ = x_sample.shape
    depth = w_in.shape[0]
    n_heads_a = sinks.shape[1]
    n_kv_a = cache_a_kv.shape[4]
    grp_a = n_heads_a // n_kv_a
    n_heads_b = cache_b1_kv.shape[4]
    w_qa, w_ka, w_b = n_heads_a * HEAD_DIM, n_kv_a * HEAD_DIM, n_heads_b * HEAD_DIM
    in_width = w_in.shape[2]
    assert in_width == w_qa + 2 * w_ka + 9 * w_b
    caches_b = (cache_b1_kv, cache_b2_kv, cache_b3_kv)

    n_p, n_s = batch * seq, n_req * t_new
    n_tok = n_p + n_s
    tm = n_tok // 8
    assert n_tok % (8 * 32) == 0

    x = jnp.concatenate([x_prompt.reshape(n_p, d_model), x_sample.reshape(n_s, d_model)], axis=0)
    positions = jnp.concatenate([jnp.tile(jnp.arange(seq), batch),
                                 jnp.tile(PAST_LEN + jnp.arange(t_new), n_req)])
    cos_t, sin_t = _rope_tables(positions)

    tn_qkv = 512
    seg = [(w_qa, True), (w_ka, True), (w_ka, False)] + [(w_b, True), (w_b, True), (w_b, False)] * 3
    rope_flags = tuple(int(f) for w, f in seg for _ in range(w // tn_qkv))
    assert len(rope_flags) == in_width // tn_qkv
    col_b = tuple(w_qa + 2 * w_ka + 3 * w_b * g for g in range(3))

    new_p = [None] * (1 + len(DILATED_PAIRS))
    new_s = None
    tm_norm = n_s
    for l in range(depth):
        h = _rmsnorm(x, norm_ffn1[l], BF16, tm_norm)
        x = _ffn(x, h, w_ffn1_in, w_ffn1_out, l, tm)

        h = _rmsnorm(x, norm_mix[l], BF16, tm_norm)
        qkv = _matmul([h], [(w_in, l, 0, 0)],
                      [(cos_t, (tm, HEAD_DIM), lambda j, i: (i, 0)), (sin_t, (tm, HEAD_DIM), lambda j, i: (i, 0))],
                      functools.partial(_rope_epilogue, tn=tn_qkv, rope_flags=rope_flags),
                      in_width, F32, tm, tn_qkv, "qkv_rope")

        oa_p = _swa_prompt(qkv, batch, seq, n_kv_a, grp_a, WINDOW_A - 1, sinks[l])
        ob_p = _dilated_prompt(qkv, batch, seq, col_b, n_heads_b, DILATED_PAIRS)
        oa_s, ob_s, new_s = _sample_step(qkv, n_p, cache_a_kv, caches_b, sinks[l], new_s, l,
                                         DILATED_PAIRS, col_b, t_new)
        new_p[0] = _export_window(qkv, new_p[0], l, depth, batch, seq, min(WINDOW_A, seq), w_qa, n_kv_a)
        for g, (window, dil) in enumerate(DILATED_PAIRS):
            new_p[1 + g] = _export_window(qkv, new_p[1 + g], l, depth, batch, seq, min(window, seq),
                                          col_b[g] + w_b, n_heads_b)
        o_a = jnp.concatenate([oa_p, oa_s], axis=0)
        o_b = jnp.concatenate([ob_p, ob_s], axis=0)

        tn_g = 256
        u = _matmul([h, o_a, o_b],
                    [(w_gate, l, 0, 0), (w_gate, l, 0, d_model // tn_g), (w_up_a, l, 1, 0), (w_up_b, l, 2, 0)],
                    [], _gate_epilogue, d_model, BF16, tm // 2, tn_g, "gate_up")
        x = _matmul([u], [(w_o, l, 0, 0)], [(x, (tm, 512), lambda j, i: (i, j))],
                    functools.partial(_residual_epilogue, scale=1.0), d_model, F32, tm, 512, "w_o")

        h = _rmsnorm(x, norm_ffn2[l], BF16, tm_norm)
        x = _ffn(x, h, w_ffn2_in, w_ffn2_out, l, tm)

    y_prompt = _rmsnorm(x, norm_final, F32, tm_norm, 0, n_p).reshape(batch, seq, d_model)
    y_sample = _rmsnorm(x, norm_final, F32, tm_norm, n_p, n_s).reshape(n_req, t_new, d_model)
    return (y_prompt, y_sample, new_p[0], new_s[0], new_p[1], new_s[1], new_p[2], new_s[2], new_p[3], new_s[3])
```

```python
import functools

import jax
import jax.numpy as jnp
from jax import lax
from jax.experimental import pallas as pl
from jax.experimental.pallas import tpu as pltpu

HEAD_DIM = 128
BLOCK = 128
WINDOW_A = 128
PAST_LEN = 16384
DILATED_PAIRS = ((128, 1), (512, 4), (2048, 16))
ROPE_THETA = 10000.0
EPS = 1e-5
NEG = -1e30
VMEM_LIMIT = 56 * 1024 * 1024

F32 = jnp.float32
BF16 = jnp.bfloat16


def _params(n_axes, vmem=VMEM_LIMIT):
    return pltpu.CompilerParams(dimension_semantics=("arbitrary",) * n_axes, vmem_limit_bytes=vmem)


def _dot(a, b):
    return jnp.dot(a, b, preferred_element_type=F32)


def _dot_nt(a, b):
    return lax.dot_general(a, b, (((1,), (1,)), ((), ())), preferred_element_type=F32)


def _head(h):
    return slice(h * HEAD_DIM, (h + 1) * HEAD_DIM)


def _rmsnorm_kernel(x_ref, g_ref, o_ref):
    x = x_ref[...]
    y = x * lax.rsqrt(jnp.mean(x * x, axis=-1, keepdims=True) + EPS)
    o_ref[...] = (y * g_ref[...]).astype(o_ref.dtype)


def _rmsnorm(x, g, out_dtype, tm, row0=0, rows=None):
    d = x.shape[1]
    rows = x.shape[0] if rows is None else rows
    assert rows % tm == 0 and row0 % tm == 0
    blk0 = row0 // tm
    return pl.pallas_call(
        _rmsnorm_kernel,
        grid=(rows // tm,),
        in_specs=[pl.BlockSpec((tm, d), lambda i: (blk0 + i, 0)), pl.BlockSpec((1, d), lambda i: (0, 0))],
        out_specs=pl.BlockSpec((tm, d), lambda i: (i, 0)),
        out_shape=jax.ShapeDtypeStruct((rows, d), out_dtype),
        compiler_params=_params(1),
        name="rmsnorm",
    )(x, g.reshape(1, d))


def _mm_kernel(*refs, n_a, b_to_a, b_is_f32, n_extra, has_side, epilogue, n_split):
    n_b = len(b_to_a)
    a_refs = refs[:n_a]
    b_refs = refs[n_a:n_a + n_b]
    pos = n_a + n_b
    extra_refs = refs[pos:pos + n_extra]
    pos += n_extra
    side_in = refs[pos] if has_side else None
    pos += int(has_side)
    o_ref = refs[pos]
    side_out = refs[pos + 1] if has_side else None
    scratch = list(refs[pos + 1 + int(has_side):])
    w_refs = [scratch.pop(0) if f32 else b_ref for b_ref, f32 in zip(b_refs, b_is_f32)]

    @pl.when(pl.program_id(1) == 0)
    def _():
        for b_ref, w_ref, f32 in zip(b_refs, w_refs, b_is_f32):
            if f32:
                w_ref[...] = b_ref[...].astype(BF16)
        if has_side:
            side_out[...] = side_in[...].astype(BF16)

    rows_per = o_ref.shape[0] // n_split
    for s in range(n_split):
        rows = pl.ds(s * rows_per, rows_per)
        accs = [_dot(a_refs[ai][rows, :], w_refs[k][...]) for k, ai in enumerate(b_to_a)]
        out = epilogue(accs, [e[rows, :] for e in extra_refs], pl.program_id(0))
        o_ref[rows, :] = out.astype(o_ref.dtype)


def _matmul(a_list, b_list, extras, epilogue, n_out, out_dtype, tm, tn, name, n_split=1, side=None):
    m = a_list[0].shape[0]
    n_j = n_out // tn
    assert m % tm == 0 and n_out % tn == 0 and tm % (16 * n_split) == 0
    in_specs = [pl.BlockSpec((tm, a.shape[1]), lambda j, i: (i, 0)) for a in a_list]
    scratch = []
    for w, layer, ai, off in b_list:
        k = a_list[ai].shape[1]
        if w.ndim == 3:
            assert w.shape[1] == k and w.shape[2] % tn == 0
            in_specs.append(pl.BlockSpec((None, k, tn), lambda j, i, layer=layer, off=off: (layer, 0, j + off)))
            scratch.append(pltpu.VMEM((k, tn), BF16))
        else:
            assert w.dtype == BF16 and w.shape[0] == k and w.shape[1] % tn == 0
            in_specs.append(pl.BlockSpec((k, tn), lambda j, i, off=off: (0, j + off)))
    for arr, shape, imap in extras:
        assert shape[0] == tm
        in_specs.append(pl.BlockSpec(shape, imap))
    out_shape = [jax.ShapeDtypeStruct((m, n_out), out_dtype)]
    out_specs = [pl.BlockSpec((tm, tn), lambda j, i: (i, j))]
    args = [*a_list, *[b[0] for b in b_list], *[arr for arr, _, _ in extras]]
    if side is not None:
        w_side, layer = side
        _, r, c = w_side.shape
        assert r % n_j == 0 and (r // n_j) % 16 == 0
        in_specs.append(pl.BlockSpec((None, r // n_j, c), lambda j, i: (layer, j, 0)))
        out_shape.append(jax.ShapeDtypeStruct((r, c), BF16))
        out_specs.append(pl.BlockSpec((r // n_j, c), lambda j, i: (j, 0)))
        args.append(w_side)
    kernel = functools.partial(_mm_kernel, n_a=len(a_list), b_to_a=tuple(b[2] for b in b_list),
                               b_is_f32=tuple(b[0].ndim == 3 for b in b_list), n_extra=len(extras),
                               has_side=side is not None, epilogue=epilogue, n_split=n_split)
    outs = pl.pallas_call(
        kernel,
        grid=(n_j, m // tm),
        in_specs=in_specs,
        out_specs=out_specs,
        out_shape=out_shape,
        scratch_shapes=scratch,
        compiler_params=_params(2),
        name=name,
    )(*args)
    return outs[0] if side is None else outs


def _swiglu_epilogue(accs, extras, j):
    a, b = accs
    return a * jax.nn.sigmoid(a) * b


def _residual_epilogue(accs, extras, j, *, scale):
    return extras[0] + scale * accs[0]


def _gate_epilogue(accs, extras, j):
    g_a, g_b, u_a, u_b = accs
    return jax.nn.sigmoid(g_a) * u_a + jax.nn.sigmoid(g_b) * u_b


def _rope_epilogue(accs, extras, j, *, tn, rope_flags):
    acc = accs[0]
    cos, sin = extras
    assert len(rope_flags) < 31
    flag_bits = sum(int(f) << idx for idx, f in enumerate(rope_flags))
    bit = lax.shift_right_logical(jnp.int32(flag_bits), j) & 1
    on = jnp.broadcast_to(bit, (1, HEAD_DIM)).astype(F32)
    cos = on * cos + (1.0 - on)
    sin = on * sin
    return jnp.concatenate(
        [acc[:, _head(h)] * cos + pltpu.roll(acc[:, _head(h)], HEAD_DIM // 2, 1) * sin
         for h in range(tn // HEAD_DIM)], axis=1)


def _ffn(x, h, w_in, w_out, layer, tm):
    tf = 256
    d_ff = w_out.shape[1]
    assert d_ff % tf == 0
    g, w_out_bf16 = _matmul([h], [(w_in, layer, 0, 0), (w_in, layer, 0, d_ff // tf)], [], _swiglu_epilogue,
                            d_ff, BF16, tm, tf, "ffn_in", n_split=2, side=(w_out, layer))
    tn = 512
    return _matmul([g], [(w_out_bf16, 0, 0, 0)], [(x, (tm // 2, tn), lambda j, i: (i, j))],
                   functools.partial(_residual_epilogue, scale=0.5), x.shape[1], F32, tm // 2, tn, "ffn_out")


def _softmax_pv(s, valid, v, sink=None):
    s = jnp.where(valid, s, NEG)
    m = jnp.max(s, axis=-1, keepdims=True)
    if sink is not None:
        m = jnp.maximum(m, sink)
    e = jnp.exp(s - m)
    l = jnp.sum(e, axis=-1, keepdims=True)
    denom = l if sink is None else l + jnp.exp(sink - m)
    return _dot(e.astype(BF16), v) / denom, m + jnp.log(l)


def _swa_prompt_kernel(q_ref, kc_ref, kp_ref, vc_ref, vp_ref, sink_ref, o_ref, *, n_kv, grp, max_dist):
    first_key = jnp.where(pl.program_id(1) == 0, BLOCK, 0)
    rows = grp * BLOCK
    qi = lax.broadcasted_iota(jnp.int32, (rows, 2 * BLOCK), 0) & (BLOCK - 1)
    kj = lax.broadcasted_iota(jnp.int32, (rows, 2 * BLOCK), 1)
    dist = qi + BLOCK - kj
    valid = (dist >= 0) & (dist <= max_dist) & (kj >= first_key)
    scale = HEAD_DIM ** -0.5
    for kh in range(n_kv):
        q = jnp.concatenate([q_ref[:, _head(kh * grp + g)] for g in range(grp)], axis=0).astype(BF16)
        k = jnp.concatenate([kp_ref[:, _head(kh)], kc_ref[:, _head(kh)]], axis=0).astype(BF16)
        v = jnp.concatenate([vp_ref[:, _head(kh)], vc_ref[:, _head(kh)]], axis=0).astype(BF16)
        sink = jnp.concatenate(
            [jnp.full((BLOCK, 1), sink_ref[kh * grp + g], F32) for g in range(grp)], axis=0)
        o, _ = _softmax_pv(_dot_nt(q, k) * scale, valid, v, sink)
        for g in range(grp):
            o_ref[:, _head(kh * grp + g)] = o[g * BLOCK:(g + 1) * BLOCK].astype(o_ref.dtype)


def _swa_prompt(qkv, n_batch, seq, n_kv, grp, max_dist, sink):
    wq, wk = n_kv * grp * HEAD_DIM, n_kv * HEAD_DIM
    assert seq % BLOCK == 0 and wq % wk == 0
    nbl = seq // BLOCK
    k_blk, v_blk = wq // wk, wq // wk + 1

    def imap(col, prev):
        return lambda n, b: (n * nbl + (jnp.maximum(b - 1, 0) if prev else b), col)

    return pl.pallas_call(
        functools.partial(_swa_prompt_kernel, n_kv=n_kv, grp=grp, max_dist=max_dist),
        grid=(n_batch, nbl),
        in_specs=[pl.BlockSpec((BLOCK, wq), imap(0, False)),
                  pl.BlockSpec((BLOCK, wk), imap(k_blk, False)), pl.BlockSpec((BLOCK, wk), imap(k_blk, True)),
                  pl.BlockSpec((BLOCK, wk), imap(v_blk, False)), pl.BlockSpec((BLOCK, wk), imap(v_blk, True)),
                  pl.BlockSpec(memory_space=pltpu.SMEM)],
        out_specs=pl.BlockSpec((BLOCK, wq), lambda n, b: (n * nbl + b, 0)),
        out_shape=jax.ShapeDtypeStruct((n_batch * seq, wq), BF16),
        compiler_params=_params(2),
        name="swa_prompt",
    )(qkv, qkv, qkv, qkv, qkv, sink)


def _dilated_prompt_kernel(*refs, groups, seq):
    n_g = len(groups)
    qkv_refs = refs[:3 * n_g]
    o_ref = refs[3 * n_g]
    o_scr = refs[3 * n_g + 1:3 * n_g + 1 + n_g]
    l_scr = refs[3 * n_g + 1 + n_g:]
    scale = HEAD_DIM ** -0.5

    def band(n_keys, off, max_dist):
        qi = lax.broadcasted_iota(jnp.int32, (BLOCK, n_keys), 0)
        kj = lax.broadcasted_iota(jnp.int32, (BLOCK, n_keys), 1)
        dist = qi + off - kj
        return (dist >= 0) & (dist <= max_dist)

    for g, (max_dist, dil) in enumerate(groups):
        q_ref, k_ref, v_ref = qkv_refs[3 * g:3 * g + 3]
        valid_first, valid_next = band(BLOCK, 0, max_dist), band(2 * BLOCK, BLOCK, max_dist)

        def rows(start, dil=dil):
            return pl.ds(start, BLOCK, stride=dil) if dil > 1 else pl.ds(start, BLOCK)

        for r in range(dil):
            for b in range(seq // dil // BLOCK):
                cur = rows(r + b * BLOCK * dil)
                q = q_ref[cur, :].astype(BF16)
                if b == 0:
                    k, v, valid = k_ref[cur, :], v_ref[cur, :], valid_first
                else:
                    prev = rows(r + (b - 1) * BLOCK * dil)
                    k = jnp.concatenate([k_ref[prev, :], k_ref[cur, :]], axis=0)
                    v = jnp.concatenate([v_ref[prev, :], v_ref[cur, :]], axis=0)
                    valid = valid_next
                o, lse = _softmax_pv(_dot_nt(q, k.astype(BF16)) * scale, valid, v.astype(BF16))
                o_scr[g][cur, :] = o
                l_scr[g][cur, :] = jnp.broadcast_to(lse, (BLOCK, HEAD_DIM))

    chunk = 2 * BLOCK
    for c in range(seq // chunk):
        cs = pl.ds(c * chunk, chunk)
        lses = [l[cs, :] for l in l_scr]
        m = functools.reduce(jnp.maximum, lses)
        ws = [jnp.exp(l - m) for l in lses]
        tot = functools.reduce(jnp.add, ws)
        mix = functools.reduce(jnp.add, [(w / tot) * o[cs, :] for w, o in zip(ws, o_scr)])
        o_ref[cs, :] = mix.astype(o_ref.dtype)


def _dilated_prompt(qkv, n_batch, seq, col0, n_heads, pairs):
    groups = tuple((window // dil, dil) for window, dil in pairs)
    assert all(seq % (dil * BLOCK) == 0 and window // dil <= BLOCK for window, dil in pairs)
    w = n_heads * HEAD_DIM
    in_specs = []
    for c in col0:
        assert c % HEAD_DIM == 0
        for part in range(3):
            cb = (c + part * w) // HEAD_DIM
            in_specs.append(pl.BlockSpec((seq, HEAD_DIM), lambda n, h, cb=cb: (n, cb + h)))
    n_g = len(groups)
    return pl.pallas_call(
        functools.partial(_dilated_prompt_kernel, groups=groups, seq=seq),
        grid=(n_batch, n_heads),
        in_specs=in_specs,
        out_specs=pl.BlockSpec((seq, HEAD_DIM), lambda n, h: (n, h)),
        out_shape=jax.ShapeDtypeStruct((n_batch * seq, w), BF16),
        scratch_shapes=[pltpu.VMEM((seq, HEAD_DIM), F32) for _ in range(2 * n_g)],
        compiler_params=_params(2),
        name="dilated_prompt",
    )(*([qkv] * (3 * n_g)))


def _export_kernel(k_ref, v_ref, prev_ref, o_ref, *, n_kv):
    del prev_ref
    for h in range(n_kv):
        o_ref[:, 0, h, :] = k_ref[:, _head(h)]
        o_ref[:, 1, h, :] = v_ref[:, _head(h)]


def _export_window(qkv, prev, layer, depth, n_batch, seq, lw, k_col, n_kv):
    wk = n_kv * HEAD_DIM
    rb = min(lw, 256)
    assert lw % rb == 0 and seq % rb == 0 and k_col % wk == 0
    shape = (depth, n_batch, lw, 2, n_kv, HEAD_DIM)
    first = prev is None
    if first:
        prev = jnp.zeros((8, HEAD_DIM), F32)
    row_blk0 = (seq - lw) // rb

    def imap(part):
        return lambda n, c: (n * (seq // rb) + row_blk0 + c, k_col // wk + part)

    return pl.pallas_call(
        functools.partial(_export_kernel, n_kv=n_kv),
        grid=(n_batch, lw // rb),
        in_specs=[pl.BlockSpec((rb, wk), imap(0)), pl.BlockSpec((rb, wk), imap(1)),
                  pl.BlockSpec(memory_space=pl.ANY)],
        out_specs=pl.BlockSpec((None, None, rb, 2, n_kv, HEAD_DIM), lambda n, c: (layer, n, c, 0, 0, 0)),
        out_shape=jax.ShapeDtypeStruct(shape, F32),
        input_output_aliases={} if first else {2: 0},
        compiler_params=_params(2),
        name="export_window",
    )(qkv, qkv, prev)


def _sample_group(q_ref, q_col, new_ref, k_col, c_ref, *, n_kv, grp, dil, lb, max_dist, sink, t_new):
    n_q = n_kv * grp
    wk = n_kv * HEAD_DIM
    rows = n_q * t_new
    n_cache = lb // dil
    n_keys = n_cache + BLOCK
    scale = HEAD_DIM ** -0.5
    assert dil & (dil - 1) == 0 and t_new & (t_new - 1) == 0

    def lane_block(x, idx, n):
        zero = jnp.zeros_like(x)
        return jnp.concatenate([x if i == idx else zero for i in range(n)], axis=1)

    q_bd = jnp.concatenate(
        [lane_block(q_ref[:, q_col + h * HEAD_DIM:q_col + (h + 1) * HEAD_DIM], h // grp, n_kv)
         for h in range(n_q)], axis=0).astype(BF16)
    pad = jnp.zeros((BLOCK - t_new, wk), F32)
    k_new = jnp.concatenate([new_ref[:, k_col:k_col + wk], pad], axis=0)
    v_new = jnp.concatenate([new_ref[:, k_col + wk:k_col + 2 * wk], pad], axis=0)

    t_row = lax.broadcasted_iota(jnp.int32, (rows, n_keys), 0) & (t_new - 1)
    lane = lax.broadcasted_iota(jnp.int32, (rows, n_keys), 1)
    t_col = lax.broadcasted_iota(jnp.int32, (rows, 1), 0) & (t_new - 1)

    o_all = jnp.zeros((rows, wk), F32)
    lse_all = jnp.zeros((rows, 1), F32)
    for r in range(min(dil, t_new)):
        k_c = jnp.concatenate([c_ref[:, 2 * r, h, :] for h in range(n_kv)], axis=1)
        v_c = jnp.concatenate([c_ref[:, 2 * r + 1, h, :] for h in range(n_kv)], axis=1)
        k = jnp.concatenate([k_c, k_new], axis=0).astype(BF16)
        v = jnp.concatenate([v_c, v_new], axis=0).astype(BF16)
        delta = jnp.where(lane < n_cache, lb + t_row - (r + dil * lane), t_row - (lane - n_cache))
        valid = ((delta >= 0) & (delta <= max_dist) & ((delta & (dil - 1)) == 0)
                 & (lane < n_cache + t_new))
        o, lse = _softmax_pv(_dot_nt(q_bd, k) * scale, valid, v, sink)
        mine = ((t_col + lb) & (dil - 1)) == r
        o_all = jnp.where(mine, o, o_all)
        lse_all = jnp.where(mine, lse, lse_all)
    return o_all, lse_all


def _sample_kernel(*refs, cfg):
    t_new, n_kv_a, grp_a, n_heads_b, pairs, lbs, lb_a, col_b = (
        cfg["t_new"], cfg["n_kv_a"], cfg["grp_a"], cfg["n_heads_b"], cfg["pairs"], cfg["lbs"],
        cfg["lb_a"], cfg["col_b"])
    n_g = len(pairs)
    qkv_ref, ca_ref = refs[0], refs[1]
    cb_refs = refs[2:2 + n_g]
    sink_ref = refs[2 + n_g]
    pos = 3 + n_g + (1 + n_g)
    oa_ref, ob_ref, na_ref = refs[pos], refs[pos + 1], refs[pos + 2]
    nb_refs = refs[pos + 3:pos + 3 + n_g]

    n_q_a = n_kv_a * grp_a
    w_qa, w_ka, w_b = n_q_a * HEAD_DIM, n_kv_a * HEAD_DIM, n_heads_b * HEAD_DIM

    sink = jnp.concatenate([jnp.full((t_new, 1), sink_ref[h], F32) for h in range(n_q_a)], axis=0)
    o, _ = _sample_group(qkv_ref, 0, qkv_ref, w_qa, ca_ref, n_kv=n_kv_a, grp=grp_a, dil=1, lb=lb_a,
                         max_dist=WINDOW_A - 1, sink=sink, t_new=t_new)
    for h in range(n_q_a):
        oa_ref[:, _head(h)] = o[h * t_new:(h + 1) * t_new, _head(h // grp_a)].astype(oa_ref.dtype)
    for h in range(n_kv_a):
        na_ref[:, 0, h, :] = qkv_ref[:, w_qa + h * HEAD_DIM:w_qa + (h + 1) * HEAD_DIM]
        na_ref[:, 1, h, :] = qkv_ref[:, w_qa + w_ka + h * HEAD_DIM:w_qa + w_ka + (h + 1) * HEAD_DIM]

    outs, lses = [], []
    for g, (window, dil) in enumerate(pairs):
        c0 = col_b[g]
        o, lse = _sample_group(qkv_ref, c0, qkv_ref, c0 + w_b, cb_refs[g], n_kv=n_heads_b, grp=1, dil=dil,
                               lb=lbs[g], max_dist=window, sink=None, t_new=t_new)
        outs.append(o)
        lses.append(lse)
        for h in range(n_heads_b):
            nb_refs[g][:, 0, h, :] = qkv_ref[:, c0 + w_b + h * HEAD_DIM:c0 + w_b + (h + 1) * HEAD_DIM]
            nb_refs[g][:, 1, h, :] = qkv_ref[:, c0 + 2 * w_b + h * HEAD_DIM:c0 + 2 * w_b + (h + 1) * HEAD_DIM]
    m = functools.reduce(jnp.maximum, lses)
    ws = [jnp.exp(l - m) for l in lses]
    tot = functools.reduce(jnp.add, ws)
    mix = functools.reduce(jnp.add, [(w / tot) * o for w, o in zip(ws, outs)])
    for h in range(n_heads_b):
        ob_ref[:, _head(h)] = mix[h * t_new:(h + 1) * t_new, _head(h)].astype(ob_ref.dtype)


def _sample_step(qkv, row0, cache_a, caches_b, sink, prev_new, layer, pairs, col_b, t_new):
    depth, n_req, lb_a, _, n_kv_a, _ = cache_a.shape
    n_q_a = sink.shape[0]
    n_heads_b = caches_b[0].shape[4]
    in_width = qkv.shape[1]
    assert row0 % t_new == 0 and t_new % 8 == 0
    lbs = tuple(c.shape[2] for c in caches_b)
    cfg = dict(t_new=t_new, n_kv_a=n_kv_a, grp_a=n_q_a // n_kv_a, n_heads_b=n_heads_b, pairs=pairs,
               lbs=lbs, lb_a=lb_a, col_b=col_b)

    def cache_spec(cache, lb, dil):
        n_kv = cache.shape[4]
        assert lb % dil == 0 and lb // dil == BLOCK
        view = cache.reshape(depth, n_req, lb // dil, dil * 2, n_kv, HEAD_DIM)
        n_cls = min(dil, t_new)
        return view, pl.BlockSpec((None, None, lb // dil, n_cls * 2, n_kv, HEAD_DIM),
                                  lambda b: (layer, b, 0, 0, 0, 0))

    args = [qkv]
    in_specs = [pl.BlockSpec((t_new, in_width), lambda b: (row0 // t_new + b, 0))]
    view, spec = cache_spec(cache_a, lb_a, 1)
    args.append(view)
    in_specs.append(spec)
    for cache, lb, (window, dil) in zip(caches_b, lbs, pairs):
        view, spec = cache_spec(cache, lb, dil)
        args.append(view)
        in_specs.append(spec)
    args.append(sink)
    in_specs.append(pl.BlockSpec(memory_space=pltpu.SMEM))

    new_shapes = [(depth, n_req, t_new, 2, n_kv_a, HEAD_DIM)] + [(depth, n_req, t_new, 2, n_heads_b, HEAD_DIM)] * len(pairs)
    first = prev_new is None
    if first:
        prev_new = [jnp.zeros((8, HEAD_DIM), F32) for _ in new_shapes]
    n_in = len(args)
    args += list(prev_new)
    in_specs += [pl.BlockSpec(memory_space=pl.ANY)] * len(prev_new)

    n_s = n_req * t_new
    out_shape = [jax.ShapeDtypeStruct((n_s, n_q_a * HEAD_DIM), BF16),
                 jax.ShapeDtypeStruct((n_s, n_heads_b * HEAD_DIM), BF16)]
    out_specs = [pl.BlockSpec((t_new, n_q_a * HEAD_DIM), lambda b: (b, 0)),
                 pl.BlockSpec((t_new, n_heads_b * HEAD_DIM), lambda b: (b, 0))]
    for s in new_shapes:
        out_shape.append(jax.ShapeDtypeStruct(s, F32))
        out_specs.append(pl.BlockSpec((None, None) + s[2:], lambda b: (layer, b, 0, 0, 0, 0)))
    aliases = {} if first else {n_in + i: 2 + i for i in range(len(new_shapes))}
    outs = pl.pallas_call(
        functools.partial(_sample_kernel, cfg=cfg),
        grid=(n_req,),
        in_specs=in_specs,
        out_specs=out_specs,
        out_shape=out_shape,
        input_output_aliases=aliases,
        compiler_params=_params(1),
        name="sample_step",
    )(*args)
    return outs[0], outs[1], list(outs[2:])


def _rope_tables(positions):
    half = HEAD_DIM // 2
    inv = ROPE_THETA ** (-jnp.arange(half, dtype=F32) / half)
    ang = positions.astype(F32)[:, None] * inv[None, :]
    cos, sin = jnp.cos(ang), jnp.sin(ang)
    return jnp.concatenate([cos, cos], axis=1), jnp.concatenate([-sin, sin], axis=1)


def kernel(x_prompt, x_sample, cache_a_kv, cache_b1_kv, cache_b2_kv, cache_b3_kv, norm_ffn1, w_ffn1_in,
           w_ffn1_out, norm_mix, w_in, sinks, w_gate, w_up_a, w_up_b, w_o, norm_ffn2, w_ffn2_in,
           w_ffn2_out, norm_final):
    batch, seq, d_model = x_prompt.shape
    n_req, t_new, _ = x_sample.shape
    depth = w_in.shape[0]
    n_heads_a = sinks.shape[1]
    n_kv_a = cache_a_kv.shape[4]
    grp_a = n_heads_a // n_kv_a
    n_heads_b = cache_b1_kv.shape[4]
    w_qa, w_ka, w_b = n_heads_a * HEAD_DIM, n_kv_a * HEAD_DIM, n_heads_b * HEAD_DIM
    in_width = w_in.shape[2]
    assert in_width == w_qa + 2 * w_ka + 9 * w_b
    caches_b = (cache_b1_kv, cache_b2_kv, cache_b3_kv)

    n_p, n_s = batch * seq, n_req * t_new
    n_tok = n_p + n_s
    tm = n_tok // 8
    assert n_tok % (8 * 32) == 0

    x = jnp.concatenate([x_prompt.reshape(n_p, d_model), x_sample.reshape(n_s, d_model)], axis=0)
    positions = jnp.concatenate([jnp.tile(jnp.arange(seq), batch),
                                 jnp.tile(PAST_LEN + jnp.arange(t_new), n_req)])
    cos_t, sin_t = _rope_tables(positions)

    tn_qkv = 512
    seg = [(w_qa, True), (w_ka, True), (w_ka, False)] + [(w_b, True), (w_b, True), (w_b, False)] * 3
    rope_flags = tuple(int(f) for w, f in seg for _ in range(w // tn_qkv))
    assert len(rope_flags) == in_width // tn_qkv
    col_b = tuple(w_qa + 2 * w_ka + 3 * w_b * g for g in range(3))

    new_p = [None] * (1 + len(DILATED_PAIRS))
    new_s = None
    tm_norm = n_s
    for l in range(depth):
        h = _rmsnorm(x, norm_ffn1[l], BF16, tm_norm)
        x = _ffn(x, h, w_ffn1_in, w_ffn1_out, l, tm)

        h = _rmsnorm(x, norm_mix[l], BF16, tm_norm)
        qkv = _matmul([h], [(w_in, l, 0, 0)],
                      [(cos_t, (tm, HEAD_DIM), lambda j, i: (i, 0)), (sin_t, (tm, HEAD_DIM), lambda j, i: (i, 0))],
                      functools.partial(_rope_epilogue, tn=tn_qkv, rope_flags=rope_flags),
                      in_width, F32, tm, tn_qkv, "qkv_rope", n_split=2)

        oa_p = _swa_prompt(qkv, batch, seq, n_kv_a, grp_a, WINDOW_A - 1, sinks[l])
        ob_p = _dilated_prompt(qkv, batch, seq, col_b, n_heads_b, DILATED_PAIRS)
        oa_s, ob_s, new_s = _sample_step(qkv, n_p, cache_a_kv, caches_b, sinks[l], new_s, l,
                                         DILATED_PAIRS, col_b, t_new)
        new_p[0] = _export_window(qkv, new_p[0], l, depth, batch, seq, min(WINDOW_A, seq), w_qa, n_kv_a)
        for g, (window, dil) in enumerate(DILATED_PAIRS):
            new_p[1 + g] = _export_window(qkv, new_p[1 + g], l, depth, batch, seq, min(window, seq),
                                          col_b[g] + w_b, n_heads_b)
        o_a = jnp.concatenate([oa_p, oa_s], axis=0)
        o_b = jnp.concatenate([ob_p, ob_s], axis=0)

        tn_g = 256
        u = _matmul([h, o_a, o_b],
                    [(w_gate, l, 0, 0), (w_gate, l, 0, d_model // tn_g), (w_up_a, l, 1, 0), (w_up_b, l, 2, 0)],
                    [], _gate_epilogue, d_model, BF16, tm // 2, tn_g, "gate_up")
        x = _matmul([u], [(w_o, l, 0, 0)], [(x, (tm, 512), lambda j, i: (i, j))],
                    functools.partial(_residual_epilogue, scale=1.0), d_model, F32, tm, 512, "w_o", n_split=2)

        h = _rmsnorm(x, norm_ffn2[l], BF16, tm_norm)
        x = _ffn(x, h, w_ffn2_in, w_ffn2_out, l, tm)

    y_prompt = _rmsnorm(x, norm_final, F32, tm_norm, 0, n_p).reshape(batch, seq, d_model)
    y_sample = _rmsnorm(x, norm_final, F32, tm_norm, n_p, n_s).reshape(n_req, t_new, d_model)
    return (y_prompt, y_sample, new_p[0], new_s[0], new_p[1], new_s[1], new_p[2], new_s[2], new_p[3], new_s[3])
```

```python
import functools

import jax
import jax.numpy as jnp
from jax import lax
from jax.experimental import pallas as pl
from jax.experimental.pallas import tpu as pltpu

HEAD_DIM = 128
BLOCK = 128
WINDOW_A = 128
PAST_LEN = 16384
DILATED_PAIRS = ((128, 1), (512, 4), (2048, 16))
ROPE_THETA = 10000.0
EPS = 1e-5
NEG = -1e30
VMEM_LIMIT = 56 * 1024 * 1024

F32 = jnp.float32
BF16 = jnp.bfloat16


def _params(n_axes, vmem=VMEM_LIMIT):
    return pltpu.CompilerParams(dimension_semantics=("arbitrary",) * n_axes, vmem_limit_bytes=vmem)


def _dot(a, b):
    return jnp.dot(a, b, preferred_element_type=F32)


def _dot_nt(a, b):
    return lax.dot_general(a, b, (((1,), (1,)), ((), ())), preferred_element_type=F32)


def _head(h):
    return slice(h * HEAD_DIM, (h + 1) * HEAD_DIM)


def _rmsnorm_kernel(x_ref, g_ref, o_ref):
    x = x_ref[...]
    y = x * lax.rsqrt(jnp.mean(x * x, axis=-1, keepdims=True) + EPS)
    o_ref[...] = (y * g_ref[...]).astype(o_ref.dtype)


def _rmsnorm(x, g, out_dtype, tm, row0=0, rows=None):
    d = x.shape[1]
    rows = x.shape[0] if rows is None else rows
    assert rows % tm == 0 and row0 % tm == 0
    blk0 = row0 // tm
    return pl.pallas_call(
        _rmsnorm_kernel,
        grid=(rows // tm,),
        in_specs=[pl.BlockSpec((tm, d), lambda i: (blk0 + i, 0)), pl.BlockSpec((1, d), lambda i: (0, 0))],
        out_specs=pl.BlockSpec((tm, d), lambda i: (i, 0)),
        out_shape=jax.ShapeDtypeStruct((rows, d), out_dtype),
        compiler_params=_params(1),
        name="rmsnorm",
    )(x, g.reshape(1, d))


def _mm_kernel(*refs, n_a, b_to_a, b_is_f32, n_extra, has_side, epilogue, n_split):
    n_b = len(b_to_a)
    a_refs = refs[:n_a]
    b_refs = refs[n_a:n_a + n_b]
    pos = n_a + n_b
    extra_refs = refs[pos:pos + n_extra]
    pos += n_extra
    side_in = refs[pos] if has_side else None
    pos += int(has_side)
    o_ref = refs[pos]
    side_out = refs[pos + 1] if has_side else None
    scratch = list(refs[pos + 1 + int(has_side):])
    w_refs = [scratch.pop(0) if f32 else b_ref for b_ref, f32 in zip(b_refs, b_is_f32)]

    @pl.when(pl.program_id(1) == 0)
    def _():
        for b_ref, w_ref, f32 in zip(b_refs, w_refs, b_is_f32):
            if f32:
                w_ref[...] = b_ref[...].astype(BF16)

    if has_side:
        side_out[...] = side_in[...].astype(BF16)

    rows_per = o_ref.shape[0] // n_split
    for s in range(n_split):
        rows = pl.ds(s * rows_per, rows_per)
        accs = [_dot(a_refs[ai][rows, :], w_refs[k][...]) for k, ai in enumerate(b_to_a)]
        out = epilogue(accs, [e[rows, :] for e in extra_refs], pl.program_id(0))
        o_ref[rows, :] = out.astype(o_ref.dtype)


def _matmul(a_list, b_list, extras, epilogue, n_out, out_dtype, tm, tn, name, n_split=1, side=None):
    m = a_list[0].shape[0]
    n_j, n_i = n_out // tn, m // tm
    assert m % tm == 0 and n_out % tn == 0 and tm % (16 * n_split) == 0
    in_specs = [pl.BlockSpec((tm, a.shape[1]), lambda j, i: (i, 0)) for a in a_list]
    args = list(a_list)
    scratch = []
    for w, layer, ai, off in b_list:
        k = a_list[ai].shape[1]
        if w.ndim == 3:
            assert w.shape[1] == k and w.shape[2] % tn == 0
            in_specs.append(pl.BlockSpec((None, k, tn), lambda j, i, layer=layer, off=off: (layer, 0, j + off)))
            args.append(w)
            scratch.append(pltpu.VMEM((k, tn), BF16))
        else:
            assert w.dtype == BF16 and w.shape[0] == k and w.shape[1] % tn == 0
            in_specs.append(pl.BlockSpec((k, tn), lambda j, i, off=off: (0, j + off)))
            args.append(w)
    for arr, shape, imap in extras:
        assert shape[0] == tm
        in_specs.append(pl.BlockSpec(shape, imap))
        args.append(arr)
    out_shape = [jax.ShapeDtypeStruct((m, n_out), out_dtype)]
    out_specs = [pl.BlockSpec((tm, tn), lambda j, i: (i, j))]
    if side is not None:
        w_side, layer = side
        _, r, c = w_side.shape
        rs = r // (n_j * n_i)
        assert r % (n_j * n_i) == 0 and rs % 16 == 0
        in_specs.append(pl.BlockSpec((None, rs, c), lambda j, i: (layer, j * n_i + i, 0)))
        out_shape.append(jax.ShapeDtypeStruct((r, c), BF16))
        out_specs.append(pl.BlockSpec((rs, c), lambda j, i: (j * n_i + i, 0)))
        args.append(w_side)
    kernel = functools.partial(_mm_kernel, n_a=len(a_list), b_to_a=tuple(b[2] for b in b_list),
                               b_is_f32=tuple(b[0].ndim == 3 for b in b_list),
                               n_extra=len(extras), has_side=side is not None, epilogue=epilogue,
                               n_split=n_split)
    outs = pl.pallas_call(
        kernel,
        grid=(n_j, m // tm),
        in_specs=in_specs,
        out_specs=out_specs,
        out_shape=out_shape,
        scratch_shapes=scratch,
        compiler_params=_params(2),
        name=name,
    )(*args)
    return outs[0] if side is None else outs


def _swiglu_epilogue(accs, extras, j):
    a, b = accs
    return a * jax.nn.sigmoid(a) * b


def _residual_epilogue(accs, extras, j, *, scale):
    return extras[0] + scale * accs[0]


def _gate_epilogue(accs, extras, j):
    g_a, g_b, u_a, u_b = accs
    return jax.nn.sigmoid(g_a) * u_a + jax.nn.sigmoid(g_b) * u_b


def _rope_epilogue(accs, extras, j, *, tn, rope_flags):
    acc = accs[0]
    cos, sin = extras
    assert len(rope_flags) < 31
    flag_bits = sum(int(f) << idx for idx, f in enumerate(rope_flags))
    bit = lax.shift_right_logical(jnp.int32(flag_bits), j) & 1
    on = jnp.broadcast_to(bit, (1, HEAD_DIM)).astype(F32)
    cos = on * cos + (1.0 - on)
    sin = on * sin
    return jnp.concatenate(
        [acc[:, _head(h)] * cos + pltpu.roll(acc[:, _head(h)], HEAD_DIM // 2, 1) * sin
         for h in range(tn // HEAD_DIM)], axis=1)


def _ffn(x, h, w_in, w_out, layer, tm):
    tf = 256
    d_ff = w_out.shape[1]
    assert d_ff % tf == 0
    g, w_out_bf16 = _matmul([h], [(w_in, layer, 0, 0), (w_in, layer, 0, d_ff // tf)], [], _swiglu_epilogue,
                            d_ff, BF16, tm, tf, "ffn_in", n_split=2, side=(w_out, layer))
    tn = 512
    return _matmul([g], [(w_out_bf16, 0, 0, 0)], [(x, (tm // 2, tn), lambda j, i: (i, j))],
                   functools.partial(_residual_epilogue, scale=0.5), x.shape[1], F32, tm // 2, tn, "ffn_out")


def _softmax_pv(s, valid, v, sink=None):
    s = jnp.where(valid, s, NEG)
    m = jnp.max(s, axis=-1, keepdims=True)
    if sink is not None:
        m = jnp.maximum(m, sink)
    e = jnp.exp(s - m)
    l = jnp.sum(e, axis=-1, keepdims=True)
    denom = l if sink is None else l + jnp.exp(sink - m)
    return _dot(e.astype(BF16), v) / denom, m + jnp.log(l)


def _swa_prompt_kernel(q_ref, kc_ref, kp_ref, vc_ref, vp_ref, sink_ref, o_ref, *, n_kv, grp, max_dist):
    first_key = jnp.where(pl.program_id(1) == 0, BLOCK, 0)
    rows = grp * BLOCK
    qi = lax.broadcasted_iota(jnp.int32, (rows, 2 * BLOCK), 0) & (BLOCK - 1)
    kj = lax.broadcasted_iota(jnp.int32, (rows, 2 * BLOCK), 1)
    dist = qi + BLOCK - kj
    valid = (dist >= 0) & (dist <= max_dist) & (kj >= first_key)
    scale = HEAD_DIM ** -0.5
    for kh in range(n_kv):
        q = jnp.concatenate([q_ref[:, _head(kh * grp + g)] for g in range(grp)], axis=0).astype(BF16)
        k = jnp.concatenate([kp_ref[:, _head(kh)], kc_ref[:, _head(kh)]], axis=0).astype(BF16)
        v = jnp.concatenate([vp_ref[:, _head(kh)], vc_ref[:, _head(kh)]], axis=0).astype(BF16)
        sink = jnp.concatenate(
            [jnp.full((BLOCK, 1), sink_ref[kh * grp + g], F32) for g in range(grp)], axis=0)
        o, _ = _softmax_pv(_dot_nt(q, k) * scale, valid, v, sink)
        for g in range(grp):
            o_ref[:, _head(kh * grp + g)] = o[g * BLOCK:(g + 1) * BLOCK].astype(o_ref.dtype)


def _swa_prompt(qkv, n_batch, seq, n_kv, grp, max_dist, sink):
    wq, wk = n_kv * grp * HEAD_DIM, n_kv * HEAD_DIM
    assert seq % BLOCK == 0 and wq % wk == 0
    nbl = seq // BLOCK
    k_blk, v_blk = wq // wk, wq // wk + 1

    def imap(col, prev):
        return lambda n, b: (n * nbl + (jnp.maximum(b - 1, 0) if prev else b), col)

    return pl.pallas_call(
        functools.partial(_swa_prompt_kernel, n_kv=n_kv, grp=grp, max_dist=max_dist),
        grid=(n_batch, nbl),
        in_specs=[pl.BlockSpec((BLOCK, wq), imap(0, False)),
                  pl.BlockSpec((BLOCK, wk), imap(k_blk, False)), pl.BlockSpec((BLOCK, wk), imap(k_blk, True)),
                  pl.BlockSpec((BLOCK, wk), imap(v_blk, False)), pl.BlockSpec((BLOCK, wk), imap(v_blk, True)),
                  pl.BlockSpec(memory_space=pltpu.SMEM)],
        out_specs=pl.BlockSpec((BLOCK, wq), lambda n, b: (n * nbl + b, 0)),
        out_shape=jax.ShapeDtypeStruct((n_batch * seq, wq), BF16),
        compiler_params=_params(2),
        name="swa_prompt",
    )(qkv, qkv, qkv, qkv, qkv, sink)


def _dilated_prompt_kernel(*refs, groups, seq):
    n_g = len(groups)
    qkv_refs = refs[:3 * n_g]
    o_ref = refs[3 * n_g]
    o_scr = refs[3 * n_g + 1:3 * n_g + 1 + n_g]
    l_scr = refs[3 * n_g + 1 + n_g:]
    scale = HEAD_DIM ** -0.5

    def band(n_keys, off, max_dist):
        qi = lax.broadcasted_iota(jnp.int32, (BLOCK, n_keys), 0)
        kj = lax.broadcasted_iota(jnp.int32, (BLOCK, n_keys), 1)
        dist = qi + off - kj
        return (dist >= 0) & (dist <= max_dist)

    for g, (max_dist, dil) in enumerate(groups):
        q_ref, k_ref, v_ref = qkv_refs[3 * g:3 * g + 3]
        valid_first, valid_next = band(BLOCK, 0, max_dist), band(2 * BLOCK, BLOCK, max_dist)

        def rows(start, dil=dil):
            return pl.ds(start, BLOCK, stride=dil) if dil > 1 else pl.ds(start, BLOCK)

        for r in range(dil):
            for b in range(seq // dil // BLOCK):
                cur = rows(r + b * BLOCK * dil)
                q = q_ref[cur, :].astype(BF16)
                if b == 0:
                    k, v, valid = k_ref[cur, :], v_ref[cur, :], valid_first
                else:
                    prev = rows(r + (b - 1) * BLOCK * dil)
                    k = jnp.concatenate([k_ref[prev, :], k_ref[cur, :]], axis=0)
                    v = jnp.concatenate([v_ref[prev, :], v_ref[cur, :]], axis=0)
                    valid = valid_next
                o, lse = _softmax_pv(_dot_nt(q, k.astype(BF16)) * scale, valid, v.astype(BF16))
                o_scr[g][cur, :] = o
                l_scr[g][cur, :] = jnp.broadcast_to(lse, (BLOCK, HEAD_DIM))

    chunk = 2 * BLOCK
    for c in range(seq // chunk):
        cs = pl.ds(c * chunk, chunk)
        lses = [l[cs, :] for l in l_scr]
        m = functools.reduce(jnp.maximum, lses)
        ws = [jnp.exp(l - m) for l in lses]
        tot = functools.reduce(jnp.add, ws)
        mix = functools.reduce(jnp.add, [(w / tot) * o[cs, :] for w, o in zip(ws, o_scr)])
        o_ref[cs, :] = mix.astype(o_ref.dtype)


def _dilated_prompt(qkv, n_batch, seq, col0, n_heads, pairs):
    groups = tuple((window // dil, dil) for window, dil in pairs)
    assert all(seq % (dil * BLOCK) == 0 and window // dil <= BLOCK for window, dil in pairs)
    w = n_heads * HEAD_DIM
    in_specs = []
    for c in col0:
        assert c % HEAD_DIM == 0
        for part in range(3):
            cb = (c + part * w) // HEAD_DIM
            in_specs.append(pl.BlockSpec((seq, HEAD_DIM), lambda n, h, cb=cb: (n, cb + h)))
    n_g = len(groups)
    return pl.pallas_call(
        functools.partial(_dilated_prompt_kernel, groups=groups, seq=seq),
        grid=(n_batch, n_heads),
        in_specs=in_specs,
        out_specs=pl.BlockSpec((seq, HEAD_DIM), lambda n, h: (n, h)),
        out_shape=jax.ShapeDtypeStruct((n_batch * seq, w), BF16),
        scratch_shapes=[pltpu.VMEM((seq, HEAD_DIM), F32) for _ in range(2 * n_g)],
        compiler_params=_params(2),
        name="dilated_prompt",
    )(*([qkv] * (3 * n_g)))


def _export_kernel(k_ref, v_ref, prev_ref, o_ref, *, n_kv):
    del prev_ref
    for h in range(n_kv):
        o_ref[:, 0, h, :] = k_ref[:, _head(h)]
        o_ref[:, 1, h, :] = v_ref[:, _head(h)]


def _export_window(qkv, prev, layer, depth, n_batch, seq, lw, k_col, n_kv):
    wk = n_kv * HEAD_DIM
    rb = min(lw, 256)
    assert lw % rb == 0 and seq % rb == 0 and k_col % wk == 0
    shape = (depth, n_batch, lw, 2, n_kv, HEAD_DIM)
    first = prev is None
    if first:
        prev = jnp.zeros((8, HEAD_DIM), F32)
    row_blk0 = (seq - lw) // rb

    def imap(part):
        return lambda n, c: (n * (seq // rb) + row_blk0 + c, k_col // wk + part)

    return pl.pallas_call(
        functools.partial(_export_kernel, n_kv=n_kv),
        grid=(n_batch, lw // rb),
        in_specs=[pl.BlockSpec((rb, wk), imap(0)), pl.BlockSpec((rb, wk), imap(1)),
                  pl.BlockSpec(memory_space=pl.ANY)],
        out_specs=pl.BlockSpec((None, None, rb, 2, n_kv, HEAD_DIM), lambda n, c: (layer, n, c, 0, 0, 0)),
        out_shape=jax.ShapeDtypeStruct(shape, F32),
        input_output_aliases={} if first else {2: 0},
        compiler_params=_params(2),
        name="export_window",
    )(qkv, qkv, prev)


def _sample_group(q_ref, q_col, new_ref, k_col, c_ref, *, n_kv, grp, dil, lb, max_dist, sink, t_new):
    n_q = n_kv * grp
    wk = n_kv * HEAD_DIM
    rows = n_q * t_new
    n_cache = lb // dil
    n_keys = n_cache + BLOCK
    scale = HEAD_DIM ** -0.5
    assert dil & (dil - 1) == 0 and t_new & (t_new - 1) == 0

    def lane_block(x, idx, n):
        zero = jnp.zeros_like(x)
        return jnp.concatenate([x if i == idx else zero for i in range(n)], axis=1)

    q_bd = jnp.concatenate(
        [lane_block(q_ref[:, q_col + h * HEAD_DIM:q_col + (h + 1) * HEAD_DIM], h // grp, n_kv)
         for h in range(n_q)], axis=0).astype(BF16)
    pad = jnp.zeros((BLOCK - t_new, wk), F32)
    k_new = jnp.concatenate([new_ref[:, k_col:k_col + wk], pad], axis=0)
    v_new = jnp.concatenate([new_ref[:, k_col + wk:k_col + 2 * wk], pad], axis=0)

    t_row = lax.broadcasted_iota(jnp.int32, (rows, n_keys), 0) & (t_new - 1)
    lane = lax.broadcasted_iota(jnp.int32, (rows, n_keys), 1)
    t_col = lax.broadcasted_iota(jnp.int32, (rows, 1), 0) & (t_new - 1)

    o_all = jnp.zeros((rows, wk), F32)
    lse_all = jnp.zeros((rows, 1), F32)
    for r in range(min(dil, t_new)):
        k_c = jnp.concatenate([c_ref[:, 2 * r, h, :] for h in range(n_kv)], axis=1)
        v_c = jnp.concatenate([c_ref[:, 2 * r + 1, h, :] for h in range(n_kv)], axis=1)
        k = jnp.concatenate([k_c, k_new], axis=0).astype(BF16)
        v = jnp.concatenate([v_c, v_new], axis=0).astype(BF16)
        delta = jnp.where(lane < n_cache, lb + t_row - (r + dil * lane), t_row - (lane - n_cache))
        valid = ((delta >= 0) & (delta <= max_dist) & ((delta & (dil - 1)) == 0)
                 & (lane < n_cache + t_new))
        o, lse = _softmax_pv(_dot_nt(q_bd, k) * scale, valid, v, sink)
        mine = ((t_col + lb) & (dil - 1)) == r
        o_all = jnp.where(mine, o, o_all)
        lse_all = jnp.where(mine, lse, lse_all)
    return o_all, lse_all


def _sample_dilated_group(q_ref, new_ref, c_ref, *, n_heads, dil, lb, max_dist, t_new):
    assert n_heads == 8 and dil & (dil - 1) == 0
    h_bits = n_heads.bit_length() - 1
    n_cache = lb // dil
    n_c, n_n = n_cache * n_heads, t_new * n_heads
    scale = HEAD_DIM ** -0.5
    q_all = q_ref[...].reshape(n_n, HEAD_DIM)
    k_new = new_ref[:, 0, :, :].reshape(n_n, HEAD_DIM).astype(BF16)
    v_new = new_ref[:, 1, :, :].reshape(n_n, HEAD_DIM).astype(BF16)
    outs, lses = [None] * t_new, [None] * t_new
    for r in range(min(dil, t_new)):
        ts = [t for t in range(t_new) if (lb + t) % dil == r]
        rows = n_heads * len(ts)
        q = jnp.concatenate([q_all[t * n_heads:(t + 1) * n_heads] for t in ts], axis=0).astype(BF16)
        k_c = c_ref[:, 2 * r, :, :].reshape(n_c, HEAD_DIM).astype(BF16)
        v_c = c_ref[:, 2 * r + 1, :, :].reshape(n_c, HEAD_DIM).astype(BF16)

        def mask(n_cols, key_row):
            row = lax.broadcasted_iota(jnp.int32, (rows, n_cols), 0)
            col = lax.broadcasted_iota(jnp.int32, (rows, n_cols), 1)
            t_q = ts[0] + dil * (row >> h_bits)
            delta = lb + t_q - key_row(col >> h_bits)
            same_head = (row & (n_heads - 1)) == (col & (n_heads - 1))
            return same_head & (delta >= 0) & (delta <= max_dist) & ((delta & (dil - 1)) == 0)

        s_c = jnp.where(mask(n_c, lambda k: r + dil * k), _dot_nt(q, k_c) * scale, NEG)
        s_n = jnp.where(mask(n_n, lambda t: lb + t), _dot_nt(q, k_new) * scale, NEG)
        m = jnp.maximum(jnp.max(s_c, axis=-1, keepdims=True), jnp.max(s_n, axis=-1, keepdims=True))
        e_c, e_n = jnp.exp(s_c - m), jnp.exp(s_n - m)
        l = jnp.sum(e_c, axis=-1, keepdims=True) + jnp.sum(e_n, axis=-1, keepdims=True)
        o = (_dot(e_c.astype(BF16), v_c) + _dot(e_n.astype(BF16), v_new)) / l
        lse = m + jnp.log(l)
        for i, t in enumerate(ts):
            outs[t] = o[i * n_heads:(i + 1) * n_heads]
            lses[t] = lse[i * n_heads:(i + 1) * n_heads]
    return jnp.concatenate(outs, axis=0), jnp.concatenate(lses, axis=0)


def _sample_kernel(*refs, cfg):
    t_new, n_kv_a, grp_a, n_heads_b, pairs, lbs, lb_a, col_b = (
        cfg["t_new"], cfg["n_kv_a"], cfg["grp_a"], cfg["n_heads_b"], cfg["pairs"], cfg["lbs"],
        cfg["lb_a"], cfg["col_b"])
    n_g = len(pairs)
    qkv_ref, ca_ref = refs[0], refs[1]
    cb_refs = refs[2:2 + n_g]
    sink_ref = refs[2 + n_g]
    pos = 3 + n_g + (1 + n_g)
    oa_ref, ob_ref, na_ref = refs[pos], refs[pos + 1], refs[pos + 2]
    nb_refs = refs[pos + 3:pos + 3 + n_g]
    q_scrs = refs[pos + 3 + n_g:pos + 3 + 2 * n_g]
    mix_scr = refs[pos + 3 + 2 * n_g]

    n_q_a = n_kv_a * grp_a
    w_qa, w_ka, w_b = n_q_a * HEAD_DIM, n_kv_a * HEAD_DIM, n_heads_b * HEAD_DIM

    sink = jnp.concatenate([jnp.full((t_new, 1), sink_ref[h], F32) for h in range(n_q_a)], axis=0)
    o, _ = _sample_group(qkv_ref, 0, qkv_ref, w_qa, ca_ref, n_kv=n_kv_a, grp=grp_a, dil=1, lb=lb_a,
                         max_dist=WINDOW_A - 1, sink=sink, t_new=t_new)
    for h in range(n_q_a):
        oa_ref[:, _head(h)] = o[h * t_new:(h + 1) * t_new, _head(h // grp_a)].astype(oa_ref.dtype)
    for h in range(n_kv_a):
        na_ref[:, 0, h, :] = qkv_ref[:, w_qa + h * HEAD_DIM:w_qa + (h + 1) * HEAD_DIM]
        na_ref[:, 1, h, :] = qkv_ref[:, w_qa + w_ka + h * HEAD_DIM:w_qa + w_ka + (h + 1) * HEAD_DIM]

    outs, lses = [], []
    for g, (window, dil) in enumerate(pairs):
        c0 = col_b[g]
        for h in range(n_heads_b):
            nb_refs[g][:, 0, h, :] = qkv_ref[:, c0 + w_b + h * HEAD_DIM:c0 + w_b + (h + 1) * HEAD_DIM]
            nb_refs[g][:, 1, h, :] = qkv_ref[:, c0 + 2 * w_b + h * HEAD_DIM:c0 + 2 * w_b + (h + 1) * HEAD_DIM]
            q_scrs[g][:, h, :] = qkv_ref[:, c0 + h * HEAD_DIM:c0 + (h + 1) * HEAD_DIM]
        o, lse = _sample_dilated_group(q_scrs[g], nb_refs[g], cb_refs[g], n_heads=n_heads_b, dil=dil,
                                       lb=lbs[g], max_dist=window, t_new=t_new)
        outs.append(o)
        lses.append(lse)
    m = functools.reduce(jnp.maximum, lses)
    ws = [jnp.exp(l - m) for l in lses]
    tot = functools.reduce(jnp.add, ws)
    mix = functools.reduce(jnp.add, [(w / tot) * o for w, o in zip(ws, outs)])
    mix_scr[...] = mix.reshape(t_new, n_heads_b, HEAD_DIM)
    for h in range(n_heads_b):
        ob_ref[:, _head(h)] = mix_scr[:, h, :].astype(ob_ref.dtype)


def _sample_step(qkv, row0, cache_a, caches_b, sink, prev_new, layer, pairs, col_b, t_new):
    depth, n_req, lb_a, _, n_kv_a, _ = cache_a.shape
    n_q_a = sink.shape[0]
    n_heads_b = caches_b[0].shape[4]
    in_width = qkv.shape[1]
    assert row0 % t_new == 0 and t_new % 8 == 0
    lbs = tuple(c.shape[2] for c in caches_b)
    cfg = dict(t_new=t_new, n_kv_a=n_kv_a, grp_a=n_q_a // n_kv_a, n_heads_b=n_heads_b, pairs=pairs,
               lbs=lbs, lb_a=lb_a, col_b=col_b)

    def cache_spec(cache, lb, dil):
        n_kv = cache.shape[4]
        assert lb % dil == 0 and lb // dil == BLOCK
        view = cache.reshape(depth, n_req, lb // dil, dil * 2, n_kv, HEAD_DIM)
        n_cls = min(dil, t_new)
        return view, pl.BlockSpec((None, None, lb // dil, n_cls * 2, n_kv, HEAD_DIM),
                                  lambda b: (layer, b, 0, 0, 0, 0))

    args = [qkv]
    in_specs = [pl.BlockSpec((t_new, in_width), lambda b: (row0 // t_new + b, 0))]
    view, spec = cache_spec(cache_a, lb_a, 1)
    args.append(view)
    in_specs.append(spec)
    for cache, lb, (window, dil) in zip(caches_b, lbs, pairs):
        view, spec = cache_spec(cache, lb, dil)
        args.append(view)
        in_specs.append(spec)
    args.append(sink)
    in_specs.append(pl.BlockSpec(memory_space=pltpu.SMEM))

    new_shapes = [(depth, n_req, t_new, 2, n_kv_a, HEAD_DIM)] + [(depth, n_req, t_new, 2, n_heads_b, HEAD_DIM)] * len(pairs)
    first = prev_new is None
    if first:
        prev_new = [jnp.zeros((8, HEAD_DIM), F32) for _ in new_shapes]
    n_in = len(args)
    args += list(prev_new)
    in_specs += [pl.BlockSpec(memory_space=pl.ANY)] * len(prev_new)

    n_s = n_req * t_new
    out_shape = [jax.ShapeDtypeStruct((n_s, n_q_a * HEAD_DIM), BF16),
                 jax.ShapeDtypeStruct((n_s, n_heads_b * HEAD_DIM), BF16)]
    out_specs = [pl.BlockSpec((t_new, n_q_a * HEAD_DIM), lambda b: (b, 0)),
                 pl.BlockSpec((t_new, n_heads_b * HEAD_DIM), lambda b: (b, 0))]
    for s in new_shapes:
        out_shape.append(jax.ShapeDtypeStruct(s, F32))
        out_specs.append(pl.BlockSpec((None, None) + s[2:], lambda b: (layer, b, 0, 0, 0, 0)))
    aliases = {} if first else {n_in + i: 2 + i for i in range(len(new_shapes))}
    outs = pl.pallas_call(
        functools.partial(_sample_kernel, cfg=cfg),
        grid=(n_req,),
        in_specs=in_specs,
        out_specs=out_specs,
        out_shape=out_shape,
        input_output_aliases=aliases,
        scratch_shapes=[pltpu.VMEM((t_new, n_heads_b, HEAD_DIM), F32) for _ in range(len(pairs) + 1)],
        compiler_params=_params(1),
        name="sample_step",
    )(*args)
    return outs[0], outs[1], list(outs[2:])


def _rope_tables(positions):
    half = HEAD_DIM // 2
    inv = ROPE_THETA ** (-jnp.arange(half, dtype=F32) / half)
    ang = positions.astype(F32)[:, None] * inv[None, :]
    cos, sin = jnp.cos(ang), jnp.sin(ang)
    return jnp.concatenate([cos, cos], axis=1), jnp.concatenate([-sin, sin], axis=1)


def kernel(x_prompt, x_sample, cache_a_kv, cache_b1_kv, cache_b2_kv, cache_b3_kv, norm_ffn1, w_ffn1_in,
           w_ffn1_out, norm_mix, w_in, sinks, w_gate, w_up_a, w_up_b, w_o, norm_ffn2, w_ffn2_in,
           w_ffn2_out, norm_final):
    batch, seq, d_model = x_prompt.shape
    n_req, t_new, _ = x_sample.shape
    depth = w_in.shape[0]
    n_heads_a = sinks.shape[1]
    n_kv_a = cache_a_kv.shape[4]
    grp_a = n_heads_a // n_kv_a
    n_heads_b = cache_b1_kv.shape[4]
    w_qa, w_ka, w_b = n_heads_a * HEAD_DIM, n_kv_a * HEAD_DIM, n_heads_b * HEAD_DIM
    in_width = w_in.shape[2]
    assert in_width == w_qa + 2 * w_ka + 9 * w_b
    caches_b = (cache_b1_kv, cache_b2_kv, cache_b3_kv)

    n_p, n_s = batch * seq, n_req * t_new
    n_tok = n_p + n_s
    tm = n_tok // 8
    assert n_tok % (8 * 32) == 0

    x = jnp.concatenate([x_prompt.reshape(n_p, d_model), x_sample.reshape(n_s, d_model)], axis=0)
    positions = jnp.concatenate([jnp.tile(jnp.arange(seq), batch),
                                 jnp.tile(PAST_LEN + jnp.arange(t_new), n_req)])
    cos_t, sin_t = _rope_tables(positions)

    tn_qkv = 512
    seg = [(w_qa, True), (w_ka, True), (w_ka, False)] + [(w_b, True), (w_b, True), (w_b, False)] * 3
    rope_flags = tuple(int(f) for w, f in seg for _ in range(w // tn_qkv))
    assert len(rope_flags) == in_width // tn_qkv
    col_b = tuple(w_qa + 2 * w_ka + 3 * w_b * g for g in range(3))

    new_p = [None] * (1 + len(DILATED_PAIRS))
    new_s = None
    tm_norm = n_s
    for l in range(depth):
        h = _rmsnorm(x, norm_ffn1[l], BF16, tm_norm)
        x = _ffn(x, h, w_ffn1_in, w_ffn1_out, l, tm)

        h = _rmsnorm(x, norm_mix[l], BF16, tm_norm)
        qkv = _matmul([h], [(w_in, l, 0, 0)],
                      [(cos_t, (tm, HEAD_DIM), lambda j, i: (i, 0)), (sin_t, (tm, HEAD_DIM), lambda j, i: (i, 0))],
                      functools.partial(_rope_epilogue, tn=tn_qkv, rope_flags=rope_flags),
                      in_width, F32, tm, tn_qkv, "qkv_rope", n_split=2)

        oa_p = _swa_prompt(qkv, batch, seq, n_kv_a, grp_a, WINDOW_A - 1, sinks[l])
        ob_p = _dilated_prompt(qkv, batch, seq, col_b, n_heads_b, DILATED_PAIRS)
        oa_s, ob_s, new_s = _sample_step(qkv, n_p, cache_a_kv, caches_b, sinks[l], new_s, l,
                                         DILATED_PAIRS, col_b, t_new)
        new_p[0] = _export_window(qkv, new_p[0], l, depth, batch, seq, min(WINDOW_A, seq), w_qa, n_kv_a)
        for g, (window, dil) in enumerate(DILATED_PAIRS):
            new_p[1 + g] = _export_window(qkv, new_p[1 + g], l, depth, batch, seq, min(window, seq),
                                          col_b[g] + w_b, n_heads_b)
        o_a = jnp.concatenate([oa_p, oa_s], axis=0)
        o_b = jnp.concatenate([ob_p, ob_s], axis=0)

        tn_g = 256
        u = _matmul([h, o_a, o_b],
                    [(w_gate, l, 0, 0), (w_gate, l, 0, d_model // tn_g), (w_up_a, l, 1, 0), (w_up_b, l, 2, 0)],
                    [], _gate_epilogue, d_model, BF16, tm // 2, tn_g, "gate_up")
        x = _matmul([u], [(w_o, l, 0, 0)], [(x, (tm, 512), lambda j, i: (i, j))],
                    functools.partial(_residual_epilogue, scale=1.0), d_model, F32, tm, 512, "w_o", n_split=2)

        h = _rmsnorm(x, norm_ffn2[l], BF16, tm_norm)
        x = _ffn(x, h, w_ffn2_in, w_ffn2_out, l, tm)

    y_prompt = _rmsnorm(x, norm_final, F32, tm_norm, 0, n_p).reshape(batch, seq, d_model)
    y_sample = _rmsnorm(x, norm_final, F32, tm_norm, n_p, n_s).reshape(n_req, t_new, d_model)
    return (y_prompt, y_sample, new_p[0], new_s[0], new_p[1], new_s[1], new_p[2], new_s[2], new_p[3], new_s[3])
```

```python
import functools

import jax
import jax.numpy as jnp
from jax import lax
from jax.experimental import pallas as pl
from jax.experimental.pallas import tpu as pltpu

HEAD_DIM = 128
BLOCK = 128
WINDOW_A = 128
PAST_LEN = 16384
DILATED_PAIRS = ((128, 1), (512, 4), (2048, 16))
ROPE_THETA = 10000.0
EPS = 1e-5
NEG = -1e30
VMEM_LIMIT = 56 * 1024 * 1024

F32 = jnp.float32
BF16 = jnp.bfloat16


def _params(n_axes, vmem=VMEM_LIMIT):
    return pltpu.CompilerParams(dimension_semantics=("arbitrary",) * n_axes, vmem_limit_bytes=vmem)


def _dot(a, b):
    return jnp.dot(a, b, preferred_element_type=F32)


def _dot_nt(a, b):
    return lax.dot_general(a, b, (((1,), (1,)), ((), ())), preferred_element_type=F32)


def _head(h):
    return slice(h * HEAD_DIM, (h + 1) * HEAD_DIM)


def _rmsnorm_kernel(x_ref, g_ref, o_ref):
    x = x_ref[...]
    y = x * lax.rsqrt(jnp.mean(x * x, axis=-1, keepdims=True) + EPS)
    o_ref[...] = (y * g_ref[...]).astype(o_ref.dtype)


def _rmsnorm(x, g, out_dtype, tm, row0=0, rows=None):
    d = x.shape[1]
    rows = x.shape[0] if rows is None else rows
    assert rows % tm == 0 and row0 % tm == 0
    blk0 = row0 // tm
    return pl.pallas_call(
        _rmsnorm_kernel,
        grid=(rows // tm,),
        in_specs=[pl.BlockSpec((tm, d), lambda i: (blk0 + i, 0)), pl.BlockSpec((1, d), lambda i: (0, 0))],
        out_specs=pl.BlockSpec((tm, d), lambda i: (i, 0)),
        out_shape=jax.ShapeDtypeStruct((rows, d), out_dtype),
        compiler_params=_params(1),
        name="rmsnorm",
    )(x, g.reshape(1, d))


def _mm_kernel(*refs, n_a, b_to_a, b_is_f32, n_extra, has_side, epilogue, n_split):
    n_b = len(b_to_a)
    a_refs = refs[:n_a]
    b_refs = refs[n_a:n_a + n_b]
    pos = n_a + n_b
    extra_refs = refs[pos:pos + n_extra]
    pos += n_extra
    side_in = refs[pos] if has_side else None
    pos += int(has_side)
    o_ref = refs[pos]
    side_out = refs[pos + 1] if has_side else None
    scratch = list(refs[pos + 1 + int(has_side):])
    w_refs = [scratch.pop(0) if f32 else b_ref for b_ref, f32 in zip(b_refs, b_is_f32)]

    @pl.when(pl.program_id(1) == 0)
    def _():
        for b_ref, w_ref, f32 in zip(b_refs, w_refs, b_is_f32):
            if f32:
                w_ref[...] = b_ref[...].astype(BF16)

    if has_side:
        side_out[...] = side_in[...].astype(BF16)

    rows_per = o_ref.shape[0] // n_split
    for s in range(n_split):
        rows = pl.ds(s * rows_per, rows_per)
        accs = [_dot(a_refs[ai][rows, :], w_refs[k][...]) for k, ai in enumerate(b_to_a)]
        out = epilogue(accs, [e[rows, :] for e in extra_refs], pl.program_id(0))
        o_ref[rows, :] = out.astype(o_ref.dtype)


def _matmul(a_list, b_list, extras, epilogue, n_out, out_dtype, tm, tn, name, n_split=1, side=None):
    m = a_list[0].shape[0]
    n_j, n_i = n_out // tn, m // tm
    assert m % tm == 0 and n_out % tn == 0 and tm % (16 * n_split) == 0
    in_specs = [pl.BlockSpec((tm, a.shape[1]), lambda j, i: (i, 0)) for a in a_list]
    args = list(a_list)
    scratch = []
    for w, layer, ai, off in b_list:
        k = a_list[ai].shape[1]
        if w.ndim == 3:
            assert w.shape[1] == k and w.shape[2] % tn == 0
            s = min(len(scratch), n_i - 1)
            in_specs.append(pl.BlockSpec(
                (None, k, tn),
                lambda j, i, layer=layer, off=off, s=s:
                    (layer, 0, jnp.minimum(j + (i > s).astype(jnp.int32), n_j - 1) + off)))
            args.append(w)
            scratch.append(pltpu.VMEM((k, tn), BF16))
        else:
            assert w.dtype == BF16 and w.shape[0] == k and w.shape[1] % tn == 0
            in_specs.append(pl.BlockSpec((k, tn), lambda j, i, off=off: (0, j + off)))
            args.append(w)
    for arr, shape, imap in extras:
        assert shape[0] == tm
        in_specs.append(pl.BlockSpec(shape, imap))
        args.append(arr)
    out_shape = [jax.ShapeDtypeStruct((m, n_out), out_dtype)]
    out_specs = [pl.BlockSpec((tm, tn), lambda j, i: (i, j))]
    if side is not None:
        w_side, layer = side
        _, r, c = w_side.shape
        rs = r // (n_j * n_i)
        assert r % (n_j * n_i) == 0 and rs % 16 == 0
        in_specs.append(pl.BlockSpec((None, rs, c), lambda j, i: (layer, j * n_i + i, 0)))
        out_shape.append(jax.ShapeDtypeStruct((r, c), BF16))
        out_specs.append(pl.BlockSpec((rs, c), lambda j, i: (j * n_i + i, 0)))
        args.append(w_side)
    kernel = functools.partial(_mm_kernel, n_a=len(a_list), b_to_a=tuple(b[2] for b in b_list),
                               b_is_f32=tuple(b[0].ndim == 3 for b in b_list),
                               n_extra=len(extras), has_side=side is not None, epilogue=epilogue,
                               n_split=n_split)
    outs = pl.pallas_call(
        kernel,
        grid=(n_j, m // tm),
        in_specs=in_specs,
        out_specs=out_specs,
        out_shape=out_shape,
        scratch_shapes=scratch,
        compiler_params=_params(2),
        name=name,
    )(*args)
    return outs[0] if side is None else outs


def _swiglu_epilogue(accs, extras, j):
    a, b = accs
    return a * jax.nn.sigmoid(a) * b


def _residual_epilogue(accs, extras, j, *, scale):
    return extras[0] + scale * accs[0]


def _gate_epilogue(accs, extras, j):
    g_a, g_b, u_a, u_b = accs
    return jax.nn.sigmoid(g_a) * u_a + jax.nn.sigmoid(g_b) * u_b


def _rope_epilogue(accs, extras, j, *, tn, rope_flags):
    acc = accs[0]
    cos, sin = extras
    assert len(rope_flags) < 31
    flag_bits = sum(int(f) << idx for idx, f in enumerate(rope_flags))
    bit = lax.shift_right_logical(jnp.int32(flag_bits), j) & 1
    on = jnp.broadcast_to(bit, (1, HEAD_DIM)).astype(F32)
    cos = on * cos + (1.0 - on)
    sin = on * sin
    return jnp.concatenate(
        [acc[:, _head(h)] * cos + pltpu.roll(acc[:, _head(h)], HEAD_DIM // 2, 1) * sin
         for h in range(tn // HEAD_DIM)], axis=1)


def _ffn(x, h, w_in, w_out, layer, tm):
    tf = 256
    d_ff = w_out.shape[1]
    assert d_ff % tf == 0
    g, w_out_bf16 = _matmul([h], [(w_in, layer, 0, 0), (w_in, layer, 0, d_ff // tf)], [], _swiglu_epilogue,
                            d_ff, BF16, tm, tf, "ffn_in", n_split=2, side=(w_out, layer))
    tn = 512
    return _matmul([g], [(w_out_bf16, 0, 0, 0)], [(x, (tm // 2, tn), lambda j, i: (i, j))],
                   functools.partial(_residual_epilogue, scale=0.5), x.shape[1], F32, tm // 2, tn, "ffn_out")


def _softmax_pv(s, valid, v, sink=None):
    s = jnp.where(valid, s, NEG)
    m = jnp.max(s, axis=-1, keepdims=True)
    if sink is not None:
        m = jnp.maximum(m, sink)
    e = jnp.exp(s - m)
    l = jnp.sum(e, axis=-1, keepdims=True)
    denom = l if sink is None else l + jnp.exp(sink - m)
    return _dot(e.astype(BF16), v) / denom, m + jnp.log(l)


def _folded_band(q, k_cur, k_prev, v_cur, v_prev, max_dist, sink=None, prev_on=None):
    rows = q.shape[0]
    assert rows % BLOCK == 0 and max_dist < BLOCK
    scale = HEAD_DIM ** -0.5
    qi = lax.broadcasted_iota(jnp.int32, (rows, BLOCK), 0) & (BLOCK - 1)
    kj = lax.broadcasted_iota(jnp.int32, (rows, BLOCK), 1)
    tri = kj <= qi
    valid = jnp.where(tri, qi - kj, qi + BLOCK - kj) <= max_dist
    if prev_on is not None:
        valid = valid & (kj <= qi + prev_on * BLOCK)
    s = jnp.where(valid, jnp.where(tri, _dot_nt(q, k_cur), _dot_nt(q, k_prev)) * scale, NEG)
    m = jnp.max(s, axis=-1, keepdims=True)
    if sink is not None:
        m = jnp.maximum(m, sink)
    e = jnp.exp(s - m)
    l = jnp.sum(e, axis=-1, keepdims=True)
    o = _dot(jnp.where(tri, e, 0.0).astype(BF16), v_cur) + _dot(jnp.where(tri, 0.0, e).astype(BF16), v_prev)
    denom = l if sink is None else l + jnp.exp(sink - m)
    return o / denom, m + jnp.log(l)


def _swa_prompt_kernel(q_ref, kc_ref, kp_ref, vc_ref, vp_ref, sink_ref, o_ref, *, n_kv, grp, max_dist):
    prev_on = jnp.where(pl.program_id(1) == 0, 0, 1)
    for kh in range(n_kv):
        q = jnp.concatenate([q_ref[:, _head(kh * grp + g)] for g in range(grp)], axis=0).astype(BF16)
        sink = jnp.concatenate(
            [jnp.full((BLOCK, 1), sink_ref[kh * grp + g], F32) for g in range(grp)], axis=0)
        o, _ = _folded_band(q, kc_ref[:, _head(kh)].astype(BF16), kp_ref[:, _head(kh)].astype(BF16),
                            vc_ref[:, _head(kh)].astype(BF16), vp_ref[:, _head(kh)].astype(BF16),
                            max_dist, sink, prev_on)
        for g in range(grp):
            o_ref[:, _head(kh * grp + g)] = o[g * BLOCK:(g + 1) * BLOCK].astype(o_ref.dtype)


def _swa_prompt(qkv, n_batch, seq, n_kv, grp, max_dist, sink):
    wq, wk = n_kv * grp * HEAD_DIM, n_kv * HEAD_DIM
    assert seq % BLOCK == 0 and wq % wk == 0
    nbl = seq // BLOCK
    k_blk, v_blk = wq // wk, wq // wk + 1

    def imap(col, prev):
        return lambda n, b: (n * nbl + (jnp.maximum(b - 1, 0) if prev else b), col)

    return pl.pallas_call(
        functools.partial(_swa_prompt_kernel, n_kv=n_kv, grp=grp, max_dist=max_dist),
        grid=(n_batch, nbl),
        in_specs=[pl.BlockSpec((BLOCK, wq), imap(0, False)),
                  pl.BlockSpec((BLOCK, wk), imap(k_blk, False)), pl.BlockSpec((BLOCK, wk), imap(k_blk, True)),
                  pl.BlockSpec((BLOCK, wk), imap(v_blk, False)), pl.BlockSpec((BLOCK, wk), imap(v_blk, True)),
                  pl.BlockSpec(memory_space=pltpu.SMEM)],
        out_specs=pl.BlockSpec((BLOCK, wq), lambda n, b: (n * nbl + b, 0)),
        out_shape=jax.ShapeDtypeStruct((n_batch * seq, wq), BF16),
        compiler_params=_params(2),
        name="swa_prompt",
    )(qkv, qkv, qkv, qkv, qkv, sink)


def _dilated_prompt_kernel(*refs, groups, seq):
    n_g = len(groups)
    qkv_refs = refs[:3 * n_g]
    o_ref = refs[3 * n_g]
    o_scr = refs[3 * n_g + 1:3 * n_g + 1 + n_g]
    l_scr = refs[3 * n_g + 1 + n_g:]
    scale = HEAD_DIM ** -0.5

    def band(n_keys, off, max_dist):
        qi = lax.broadcasted_iota(jnp.int32, (BLOCK, n_keys), 0)
        kj = lax.broadcasted_iota(jnp.int32, (BLOCK, n_keys), 1)
        dist = qi + off - kj
        return (dist >= 0) & (dist <= max_dist)

    for g, (max_dist, dil) in enumerate(groups):
        q_ref, k_ref, v_ref = qkv_refs[3 * g:3 * g + 3]
        valid_first, valid_next = band(BLOCK, 0, max_dist), band(2 * BLOCK, BLOCK, max_dist)

        def rows(start, dil=dil):
            return pl.ds(start, BLOCK, stride=dil) if dil > 1 else pl.ds(start, BLOCK)

        for r in range(dil):
            for b in range(seq // dil // BLOCK):
                cur = rows(r + b * BLOCK * dil)
                q = q_ref[cur, :].astype(BF16)
                if b == 0:
                    k, v, valid = k_ref[cur, :], v_ref[cur, :], valid_first
                else:
                    prev = rows(r + (b - 1) * BLOCK * dil)
                    k = jnp.concatenate([k_ref[prev, :], k_ref[cur, :]], axis=0)
                    v = jnp.concatenate([v_ref[prev, :], v_ref[cur, :]], axis=0)
                    valid = valid_next
                o, lse = _softmax_pv(_dot_nt(q, k.astype(BF16)) * scale, valid, v.astype(BF16))
                o_scr[g][cur, :] = o
                l_scr[g][cur, :] = jnp.broadcast_to(lse, (BLOCK, HEAD_DIM))

    chunk = 2 * BLOCK
    for c in range(seq // chunk):
        cs = pl.ds(c * chunk, chunk)
        lses = [l[cs, :] for l in l_scr]
        m = functools.reduce(jnp.maximum, lses)
        ws = [jnp.exp(l - m) for l in lses]
        tot = functools.reduce(jnp.add, ws)
        mix = functools.reduce(jnp.add, [(w / tot) * o[cs, :] for w, o in zip(ws, o_scr)])
        o_ref[cs, :] = mix.astype(o_ref.dtype)


def _dilated_prompt(qkv, n_batch, seq, col0, n_heads, pairs):
    groups = tuple((window // dil, dil) for window, dil in pairs)
    assert all(seq % (dil * BLOCK) == 0 and window // dil <= BLOCK for window, dil in pairs)
    w = n_heads * HEAD_DIM
    in_specs = []
    for c in col0:
        assert c % HEAD_DIM == 0
        for part in range(3):
            cb = (c + part * w) // HEAD_DIM
            in_specs.append(pl.BlockSpec((seq, HEAD_DIM), lambda n, h, cb=cb: (n, cb + h)))
    n_g = len(groups)
    return pl.pallas_call(
        functools.partial(_dilated_prompt_kernel, groups=groups, seq=seq),
        grid=(n_batch, n_heads),
        in_specs=in_specs,
        out_specs=pl.BlockSpec((seq, HEAD_DIM), lambda n, h: (n, h)),
        out_shape=jax.ShapeDtypeStruct((n_batch * seq, w), BF16),
        scratch_shapes=[pltpu.VMEM((seq, HEAD_DIM), F32) for _ in range(2 * n_g)],
        compiler_params=_params(2),
        name="dilated_prompt",
    )(*([qkv] * (3 * n_g)))


def _export_kernel(k_ref, v_ref, prev_ref, o_ref, *, n_kv):
    del prev_ref
    for h in range(n_kv):
        o_ref[:, 0, h, :] = k_ref[:, _head(h)]
        o_ref[:, 1, h, :] = v_ref[:, _head(h)]


def _export_window(qkv, prev, layer, depth, n_batch, seq, lw, k_col, n_kv):
    wk = n_kv * HEAD_DIM
    rb = min(lw, 256)
    assert lw % rb == 0 and seq % rb == 0 and k_col % wk == 0
    shape = (depth, n_batch, lw, 2, n_kv, HEAD_DIM)
    first = prev is None
    if first:
        prev = jnp.zeros((8, HEAD_DIM), F32)
    row_blk0 = (seq - lw) // rb

    def imap(part):
        return lambda n, c: (n * (seq // rb) + row_blk0 + c, k_col // wk + part)

    return pl.pallas_call(
        functools.partial(_export_kernel, n_kv=n_kv),
        grid=(n_batch, lw // rb),
        in_specs=[pl.BlockSpec((rb, wk), imap(0)), pl.BlockSpec((rb, wk), imap(1)),
                  pl.BlockSpec(memory_space=pl.ANY)],
        out_specs=pl.BlockSpec((None, None, rb, 2, n_kv, HEAD_DIM), lambda n, c: (layer, n, c, 0, 0, 0)),
        out_shape=jax.ShapeDtypeStruct(shape, F32),
        input_output_aliases={} if first else {2: 0},
        compiler_params=_params(2),
        name="export_window",
    )(qkv, qkv, prev)


def _sample_group(q_ref, q_col, new_ref, k_col, c_ref, *, n_kv, grp, dil, lb, max_dist, sink, t_new):
    n_q = n_kv * grp
    wk = n_kv * HEAD_DIM
    rows = n_q * t_new
    n_cache = lb // dil
    n_keys = n_cache + BLOCK
    scale = HEAD_DIM ** -0.5
    assert dil & (dil - 1) == 0 and t_new & (t_new - 1) == 0

    def lane_block(x, idx, n):
        zero = jnp.zeros_like(x)
        return jnp.concatenate([x if i == idx else zero for i in range(n)], axis=1)

    q_bd = jnp.concatenate(
        [lane_block(q_ref[:, q_col + h * HEAD_DIM:q_col + (h + 1) * HEAD_DIM], h // grp, n_kv)
         for h in range(n_q)], axis=0).astype(BF16)
    pad = jnp.zeros((BLOCK - t_new, wk), F32)
    k_new = jnp.concatenate([new_ref[:, k_col:k_col + wk], pad], axis=0)
    v_new = jnp.concatenate([new_ref[:, k_col + wk:k_col + 2 * wk], pad], axis=0)

    t_row = lax.broadcasted_iota(jnp.int32, (rows, n_keys), 0) & (t_new - 1)
    lane = lax.broadcasted_iota(jnp.int32, (rows, n_keys), 1)
    t_col = lax.broadcasted_iota(jnp.int32, (rows, 1), 0) & (t_new - 1)

    o_all = jnp.zeros((rows, wk), F32)
    lse_all = jnp.zeros((rows, 1), F32)
    for r in range(min(dil, t_new)):
        k_c = jnp.concatenate([c_ref[:, 2 * r, h, :] for h in range(n_kv)], axis=1)
        v_c = jnp.concatenate([c_ref[:, 2 * r + 1, h, :] for h in range(n_kv)], axis=1)
        k = jnp.concatenate([k_c, k_new], axis=0).astype(BF16)
        v = jnp.concatenate([v_c, v_new], axis=0).astype(BF16)
        delta = jnp.where(lane < n_cache, lb + t_row - (r + dil * lane), t_row - (lane - n_cache))
        valid = ((delta >= 0) & (delta <= max_dist) & ((delta & (dil - 1)) == 0)
                 & (lane < n_cache + t_new))
        o, lse = _softmax_pv(_dot_nt(q_bd, k) * scale, valid, v, sink)
        mine = ((t_col + lb) & (dil - 1)) == r
        o_all = jnp.where(mine, o, o_all)
        lse_all = jnp.where(mine, lse, lse_all)
    return o_all, lse_all


def _sample_dilated_group(q_ref, new_ref, c_ref, *, n_heads, dil, lb, max_dist, t_new):
    assert n_heads == 8 and dil & (dil - 1) == 0
    h_bits = n_heads.bit_length() - 1
    n_cache = lb // dil
    n_c, n_n = n_cache * n_heads, t_new * n_heads
    scale = HEAD_DIM ** -0.5
    q_all = q_ref[...].reshape(n_n, HEAD_DIM)
    k_new = new_ref[:, 0, :, :].reshape(n_n, HEAD_DIM).astype(BF16)
    v_new = new_ref[:, 1, :, :].reshape(n_n, HEAD_DIM).astype(BF16)
    outs, lses = [None] * t_new, [None] * t_new
    for r in range(min(dil, t_new)):
        ts = [t for t in range(t_new) if (lb + t) % dil == r]
        rows = n_heads * len(ts)
        q = jnp.concatenate([q_all[t * n_heads:(t + 1) * n_heads] for t in ts], axis=0).astype(BF16)
        k_c = c_ref[:, 2 * r, :, :].reshape(n_c, HEAD_DIM).astype(BF16)
        v_c = c_ref[:, 2 * r + 1, :, :].reshape(n_c, HEAD_DIM).astype(BF16)

        def mask(n_cols, key_row):
            row = lax.broadcasted_iota(jnp.int32, (rows, n_cols), 0)
            col = lax.broadcasted_iota(jnp.int32, (rows, n_cols), 1)
            t_q = ts[0] + dil * (row >> h_bits)
            delta = lb + t_q - key_row(col >> h_bits)
            same_head = (row & (n_heads - 1)) == (col & (n_heads - 1))
            return same_head & (delta >= 0) & (delta <= max_dist) & ((delta & (dil - 1)) == 0)

        s_c = jnp.where(mask(n_c, lambda k: r + dil * k), _dot_nt(q, k_c) * scale, NEG)
        s_n = jnp.where(mask(n_n, lambda t: lb + t), _dot_nt(q, k_new) * scale, NEG)
        m = jnp.maximum(jnp.max(s_c, axis=-1, keepdims=True), jnp.max(s_n, axis=-1, keepdims=True))
        e_c, e_n = jnp.exp(s_c - m), jnp.exp(s_n - m)
        l = jnp.sum(e_c, axis=-1, keepdims=True) + jnp.sum(e_n, axis=-1, keepdims=True)
        o = (_dot(e_c.astype(BF16), v_c) + _dot(e_n.astype(BF16), v_new)) / l
        lse = m + jnp.log(l)
        for i, t in enumerate(ts):
            outs[t] = o[i * n_heads:(i + 1) * n_heads]
            lses[t] = lse[i * n_heads:(i + 1) * n_heads]
    return jnp.concatenate(outs, axis=0), jnp.concatenate(lses, axis=0)


def _sample_kernel(*refs, cfg):
    t_new, n_kv_a, grp_a, n_heads_b, pairs, lbs, lb_a, col_b = (
        cfg["t_new"], cfg["n_kv_a"], cfg["grp_a"], cfg["n_heads_b"], cfg["pairs"], cfg["lbs"],
        cfg["lb_a"], cfg["col_b"])
    n_g = len(pairs)
    qkv_ref, ca_ref = refs[0], refs[1]
    cb_refs = refs[2:2 + n_g]
    sink_ref = refs[2 + n_g]
    pos = 3 + n_g + (1 + n_g)
    oa_ref, ob_ref, na_ref = refs[pos], refs[pos + 1], refs[pos + 2]
    nb_refs = refs[pos + 3:pos + 3 + n_g]
    q_scrs = refs[pos + 3 + n_g:pos + 3 + 2 * n_g]
    mix_scr = refs[pos + 3 + 2 * n_g]

    n_q_a = n_kv_a * grp_a
    w_qa, w_ka, w_b = n_q_a * HEAD_DIM, n_kv_a * HEAD_DIM, n_heads_b * HEAD_DIM

    sink = jnp.concatenate([jnp.full((t_new, 1), sink_ref[h], F32) for h in range(n_q_a)], axis=0)
    o, _ = _sample_group(qkv_ref, 0, qkv_ref, w_qa, ca_ref, n_kv=n_kv_a, grp=grp_a, dil=1, lb=lb_a,
                         max_dist=WINDOW_A - 1, sink=sink, t_new=t_new)
    for h in range(n_q_a):
        oa_ref[:, _head(h)] = o[h * t_new:(h + 1) * t_new, _head(h // grp_a)].astype(oa_ref.dtype)
    for h in range(n_kv_a):
        na_ref[:, 0, h, :] = qkv_ref[:, w_qa + h * HEAD_DIM:w_qa + (h + 1) * HEAD_DIM]
        na_ref[:, 1, h, :] = qkv_ref[:, w_qa + w_ka + h * HEAD_DIM:w_qa + w_ka + (h + 1) * HEAD_DIM]

    outs, lses = [], []
    for g, (window, dil) in enumerate(pairs):
        c0 = col_b[g]
        for h in range(n_heads_b):
            nb_refs[g][:, 0, h, :] = qkv_ref[:, c0 + w_b + h * HEAD_DIM:c0 + w_b + (h + 1) * HEAD_DIM]
            nb_refs[g][:, 1, h, :] = qkv_ref[:, c0 + 2 * w_b + h * HEAD_DIM:c0 + 2 * w_b + (h + 1) * HEAD_DIM]
            q_scrs[g][:, h, :] = qkv_ref[:, c0 + h * HEAD_DIM:c0 + (h + 1) * HEAD_DIM]
        o, lse = _sample_dilated_group(q_scrs[g], nb_refs[g], cb_refs[g], n_heads=n_heads_b, dil=dil,
                                       lb=lbs[g], max_dist=window, t_new=t_new)
        outs.append(o)
        lses.append(lse)
    m = functools.reduce(jnp.maximum, lses)
    ws = [jnp.exp(l - m) for l in lses]
    tot = functools.reduce(jnp.add, ws)
    mix = functools.reduce(jnp.add, [(w / tot) * o for w, o in zip(ws, outs)])
    mix_scr[...] = mix.reshape(t_new, n_heads_b, HEAD_DIM)
    for h in range(n_heads_b):
        ob_ref[:, _head(h)] = mix_scr[:, h, :].astype(ob_ref.dtype)


def _sample_step(qkv, row0, cache_a, caches_b, sink, prev_new, layer, pairs, col_b, t_new):
    depth, n_req, lb_a, _, n_kv_a, _ = cache_a.shape
    n_q_a = sink.shape[0]
    n_heads_b = caches_b[0].shape[4]
    in_width = qkv.shape[1]
    assert row0 % t_new == 0 and t_new % 8 == 0
    lbs = tuple(c.shape[2] for c in caches_b)
    cfg = dict(t_new=t_new, n_kv_a=n_kv_a, grp_a=n_q_a // n_kv_a, n_heads_b=n_heads_b, pairs=pairs,
               lbs=lbs, lb_a=lb_a, col_b=col_b)

    def cache_spec(cache, lb, dil):
        n_kv = cache.shape[4]
        assert lb % dil == 0 and lb // dil == BLOCK
        view = cache.reshape(depth, n_req, lb // dil, dil * 2, n_kv, HEAD_DIM)
        n_cls = min(dil, t_new)
        return view, pl.BlockSpec((None, None, lb // dil, n_cls * 2, n_kv, HEAD_DIM),
                                  lambda b: (layer, b, 0, 0, 0, 0))

    args = [qkv]
    in_specs = [pl.BlockSpec((t_new, in_width), lambda b: (row0 // t_new + b, 0))]
    view, spec = cache_spec(cache_a, lb_a, 1)
    args.append(view)
    in_specs.append(spec)
    for cache, lb, (window, dil) in zip(caches_b, lbs, pairs):
        view, spec = cache_spec(cache, lb, dil)
        args.append(view)
        in_specs.append(spec)
    args.append(sink)
    in_specs.append(pl.BlockSpec(memory_space=pltpu.SMEM))

    new_shapes = [(depth, n_req, t_new, 2, n_kv_a, HEAD_DIM)] + [(depth, n_req, t_new, 2, n_heads_b, HEAD_DIM)] * len(pairs)
    first = prev_new is None
    if first:
        prev_new = [jnp.zeros((8, HEAD_DIM), F32) for _ in new_shapes]
    n_in = len(args)
    args += list(prev_new)
    in_specs += [pl.BlockSpec(memory_space=pl.ANY)] * len(prev_new)

    n_s = n_req * t_new
    out_shape = [jax.ShapeDtypeStruct((n_s, n_q_a * HEAD_DIM), BF16),
                 jax.ShapeDtypeStruct((n_s, n_heads_b * HEAD_DIM), BF16)]
    out_specs = [pl.BlockSpec((t_new, n_q_a * HEAD_DIM), lambda b: (b, 0)),
                 pl.BlockSpec((t_new, n_heads_b * HEAD_DIM), lambda b: (b, 0))]
    for s in new_shapes:
        out_shape.append(jax.ShapeDtypeStruct(s, F32))
        out_specs.append(pl.BlockSpec((None, None) + s[2:], lambda b: (layer, b, 0, 0, 0, 0)))
    aliases = {} if first else {n_in + i: 2 + i for i in range(len(new_shapes))}
    outs = pl.pallas_call(
        functools.partial(_sample_kernel, cfg=cfg),
        grid=(n_req,),
        in_specs=in_specs,
        out_specs=out_specs,
        out_shape=out_shape,
        input_output_aliases=aliases,
        scratch_shapes=[pltpu.VMEM((t_new, n_heads_b, HEAD_DIM), F32) for _ in range(len(pairs) + 1)],
        compiler_params=_params(1),
        name="sample_step",
    )(*args)
    return outs[0], outs[1], list(outs[2:])


def _rope_tables(positions):
    half = HEAD_DIM // 2
    inv = ROPE_THETA ** (-jnp.arange(half, dtype=F32) / half)
    ang = positions.astype(F32)[:, None] * inv[None, :]
    cos, sin = jnp.cos(ang), jnp.sin(ang)
    return jnp.concatenate([cos, cos], axis=1), jnp.concatenate([-sin, sin], axis=1)


def kernel(x_prompt, x_sample, cache_a_kv, cache_b1_kv, cache_b2_kv, cache_b3_kv, norm_ffn1, w_ffn1_in,
           w_ffn1_out, norm_mix, w_in, sinks, w_gate, w_up_a, w_up_b, w_o, norm_ffn2, w_ffn2_in,
           w_ffn2_out, norm_final):
    batch, seq, d_model = x_prompt.shape
    n_req, t_new, _ = x_sample.shape
    depth = w_in.shape[0]
    n_heads_a = sinks.shape[1]
    n_kv_a = cache_a_kv.shape[4]
    grp_a = n_heads_a // n_kv_a
    n_heads_b = cache_b1_kv.shape[4]
    w_qa, w_ka, w_b = n_heads_a * HEAD_DIM, n_kv_a * HEAD_DIM, n_heads_b * HEAD_DIM
    in_width = w_in.shape[2]
    assert in_width == w_qa + 2 * w_ka + 9 * w_b
    caches_b = (cache_b1_kv, cache_b2_kv, cache_b3_kv)

    n_p, n_s = batch * seq, n_req * t_new
    n_tok = n_p + n_s
    tm = n_tok // 8
    assert n_tok % (8 * 32) == 0

    x = jnp.concatenate([x_prompt.reshape(n_p, d_model), x_sample.reshape(n_s, d_model)], axis=0)
    positions = jnp.concatenate([jnp.tile(jnp.arange(seq), batch),
                                 jnp.tile(PAST_LEN + jnp.arange(t_new), n_req)])
    cos_t, sin_t = _rope_tables(positions)

    tn_qkv = 512
    seg = [(w_qa, True), (w_ka, True), (w_ka, False)] + [(w_b, True), (w_b, True), (w_b, False)] * 3
    rope_flags = tuple(int(f) for w, f in seg for _ in range(w // tn_qkv))
    assert len(rope_flags) == in_width // tn_qkv
    col_b = tuple(w_qa + 2 * w_ka + 3 * w_b * g for g in range(3))

    new_p = [None] * (1 + len(DILATED_PAIRS))
    new_s = None
    tm_norm = n_s
    for l in range(depth):
        h = _rmsnorm(x, norm_ffn1[l], BF16, tm_norm)
        x = _ffn(x, h, w_ffn1_in, w_ffn1_out, l, tm)

        h = _rmsnorm(x, norm_mix[l], BF16, tm_norm)
        qkv = _matmul([h], [(w_in, l, 0, 0)],
                      [(cos_t, (tm, HEAD_DIM), lambda j, i: (i, 0)), (sin_t, (tm, HEAD_DIM), lambda j, i: (i, 0))],
                      functools.partial(_rope_epilogue, tn=tn_qkv, rope_flags=rope_flags),
                      in_width, F32, tm, tn_qkv, "qkv_rope", n_split=2)

        oa_p = _swa_prompt(qkv, batch, seq, n_kv_a, grp_a, WINDOW_A - 1, sinks[l])
        ob_p = _dilated_prompt(qkv, batch, seq, col_b, n_heads_b, DILATED_PAIRS)
        oa_s, ob_s, new_s = _sample_step(qkv, n_p, cache_a_kv, caches_b, sinks[l], new_s, l,
                                         DILATED_PAIRS, col_b, t_new)
        new_p[0] = _export_window(qkv, new_p[0], l, depth, batch, seq, min(WINDOW_A, seq), w_qa, n_kv_a)
        for g, (window, dil) in enumerate(DILATED_PAIRS):
            new_p[1 + g] = _export_window(qkv, new_p[1 + g], l, depth, batch, seq, min(window, seq),
                                          col_b[g] + w_b, n_heads_b)
        o_a = jnp.concatenate([oa_p, oa_s], axis=0)
        o_b = jnp.concatenate([ob_p, ob_s], axis=0)

        tn_g = 256
        u = _matmul([h, o_a, o_b],
                    [(w_gate, l, 0, 0), (w_gate, l, 0, d_model // tn_g), (w_up_a, l, 1, 0), (w_up_b, l, 2, 0)],
                    [], _gate_epilogue, d_model, BF16, tm // 2, tn_g, "gate_up")
        x = _matmul([u], [(w_o, l, 0, 0)], [(x, (tm, 512), lambda j, i: (i, j))],
                    functools.partial(_residual_epilogue, scale=1.0), d_model, F32, tm, 512, "w_o", n_split=2)

        h = _rmsnorm(x, norm_ffn2[l], BF16, tm_norm)
        x = _ffn(x, h, w_ffn2_in, w_ffn2_out, l, tm)

    y_prompt = _rmsnorm(x, norm_final, F32, tm_norm, 0, n_p).reshape(batch, seq, d_model)
    y_sample = _rmsnorm(x, norm_final, F32, tm_norm, n_p, n_s).reshape(n_req, t_new, d_model)
    return (y_prompt, y_sample, new_p[0], new_s[0], new_p[1], new_s[1], new_p[2], new_s[2], new_p[3], new_s[3])
```

```python
import functools

import jax
import jax.numpy as jnp
from jax import lax
from jax.experimental import pallas as pl
from jax.experimental.pallas import tpu as pltpu

HEAD_DIM = 128
BLOCK = 128
WINDOW_A = 128
PAST_LEN = 16384
DILATED_PAIRS = ((128, 1), (512, 4), (2048, 16))
ROPE_THETA = 10000.0
EPS = 1e-5
NEG = -1e30
VMEM_LIMIT = 56 * 1024 * 1024
UNITS_IN_FLIGHT = 8

F32 = jnp.float32
BF16 = jnp.bfloat16


def _params(n_axes, vmem=VMEM_LIMIT):
    return pltpu.CompilerParams(dimension_semantics=("arbitrary",) * n_axes, vmem_limit_bytes=vmem)


def _dot(a, b):
    return jnp.dot(a, b, preferred_element_type=F32)


def _dot_nt(a, b):
    return lax.dot_general(a, b, (((1,), (1,)), ((), ())), preferred_element_type=F32)


def _head(h):
    return slice(h * HEAD_DIM, (h + 1) * HEAD_DIM)


def _rmsnorm_kernel(x_ref, g_ref, o_ref):
    x = x_ref[...]
    y = x * lax.rsqrt(jnp.mean(x * x, axis=-1, keepdims=True) + EPS)
    o_ref[...] = (y * g_ref[...]).astype(o_ref.dtype)


def _rmsnorm(x, g, out_dtype, tm, row0=0, rows=None):
    d = x.shape[1]
    rows = x.shape[0] if rows is None else rows
    assert rows % tm == 0 and row0 % tm == 0
    blk0 = row0 // tm
    return pl.pallas_call(
        _rmsnorm_kernel,
        grid=(rows // tm,),
        in_specs=[pl.BlockSpec((tm, d), lambda i: (blk0 + i, 0)), pl.BlockSpec((1, d), lambda i: (0, 0))],
        out_specs=pl.BlockSpec((tm, d), lambda i: (i, 0)),
        out_shape=jax.ShapeDtypeStruct((rows, d), out_dtype),
        compiler_params=_params(1),
        name="rmsnorm",
    )(x, g.reshape(1, d))


def _mm_kernel(*refs, n_a, b_to_a, b_is_f32, n_extra, has_side, epilogue, n_split):
    n_b = len(b_to_a)
    a_refs = refs[:n_a]
    b_refs = refs[n_a:n_a + n_b]
    pos = n_a + n_b
    extra_refs = refs[pos:pos + n_extra]
    pos += n_extra
    side_in = refs[pos] if has_side else None
    pos += int(has_side)
    o_ref = refs[pos]
    side_out = refs[pos + 1] if has_side else None
    scratch = list(refs[pos + 1 + int(has_side):])
    w_refs = [scratch.pop(0) if f32 else b_ref for b_ref, f32 in zip(b_refs, b_is_f32)]

    @pl.when(pl.program_id(1) == 0)
    def _():
        for b_ref, w_ref, f32 in zip(b_refs, w_refs, b_is_f32):
            if f32:
                w_ref[...] = b_ref[...].astype(BF16)

    if has_side:
        side_out[...] = side_in[...].astype(BF16)

    rows_per = o_ref.shape[0] // n_split
    for s in range(n_split):
        rows = pl.ds(s * rows_per, rows_per)
        accs = [_dot(a_refs[ai][rows, :], w_refs[k][...]) for k, ai in enumerate(b_to_a)]
        out = epilogue(accs, [e[rows, :] for e in extra_refs], pl.program_id(0))
        o_ref[rows, :] = out.astype(o_ref.dtype)


def _matmul(a_list, b_list, extras, epilogue, n_out, out_dtype, tm, tn, name, n_split=1, side=None):
    m = a_list[0].shape[0]
    n_j, n_i = n_out // tn, m // tm
    assert m % tm == 0 and n_out % tn == 0 and tm % (16 * n_split) == 0
    in_specs = [pl.BlockSpec((tm, a.shape[1]), lambda j, i: (i, 0)) for a in a_list]
    args = list(a_list)
    scratch = []
    for w, layer, ai, off in b_list:
        k = a_list[ai].shape[1]
        if w.ndim == 3:
            assert w.shape[1] == k and w.shape[2] % tn == 0
            s = min(len(scratch), n_i - 1)
            in_specs.append(pl.BlockSpec(
                (None, k, tn),
                lambda j, i, layer=layer, off=off, s=s:
                    (layer, 0, jnp.minimum(j + (i > s).astype(jnp.int32), n_j - 1) + off)))
            args.append(w)
            scratch.append(pltpu.VMEM((k, tn), BF16))
        else:
            assert w.dtype == BF16 and w.shape[0] == k and w.shape[1] % tn == 0
            in_specs.append(pl.BlockSpec((k, tn), lambda j, i, off=off: (0, j + off)))
            args.append(w)
    for arr, shape, imap in extras:
        assert shape[0] == tm
        in_specs.append(pl.BlockSpec(shape, imap))
        args.append(arr)
    out_shape = [jax.ShapeDtypeStruct((m, n_out), out_dtype)]
    out_specs = [pl.BlockSpec((tm, tn), lambda j, i: (i, j))]
    if side is not None:
        w_side, layer = side
        _, r, c = w_side.shape
        rs = r // (n_j * n_i)
        assert r % (n_j * n_i) == 0 and rs % 16 == 0
        in_specs.append(pl.BlockSpec((None, rs, c), lambda j, i: (layer, j * n_i + i, 0)))
        out_shape.append(jax.ShapeDtypeStruct((r, c), BF16))
        out_specs.append(pl.BlockSpec((rs, c), lambda j, i: (j * n_i + i, 0)))
        args.append(w_side)
    kernel = functools.partial(_mm_kernel, n_a=len(a_list), b_to_a=tuple(b[2] for b in b_list),
                               b_is_f32=tuple(b[0].ndim == 3 for b in b_list),
                               n_extra=len(extras), has_side=side is not None, epilogue=epilogue,
                               n_split=n_split)
    outs = pl.pallas_call(
        kernel,
        grid=(n_j, m // tm),
        in_specs=in_specs,
        out_specs=out_specs,
        out_shape=out_shape,
        scratch_shapes=scratch,
        compiler_params=_params(2),
        name=name,
    )(*args)
    return outs[0] if side is None else outs


def _swiglu_epilogue(accs, extras, j):
    a, b = accs
    return a * jax.nn.sigmoid(a) * b


def _residual_epilogue(accs, extras, j, *, scale):
    return extras[0] + scale * accs[0]


def _gate_epilogue(accs, extras, j):
    g_a, g_b, u_a, u_b = accs
    return jax.nn.sigmoid(g_a) * u_a + jax.nn.sigmoid(g_b) * u_b


def _rope_epilogue(accs, extras, j, *, tn, rope_flags):
    acc = accs[0]
    cos, sin = extras
    assert len(rope_flags) < 31
    flag_bits = sum(int(f) << idx for idx, f in enumerate(rope_flags))
    bit = lax.shift_right_logical(jnp.int32(flag_bits), j) & 1
    on = jnp.broadcast_to(bit, (1, HEAD_DIM)).astype(F32)
    cos = on * cos + (1.0 - on)
    sin = on * sin
    return jnp.concatenate(
        [acc[:, _head(h)] * cos + pltpu.roll(acc[:, _head(h)], HEAD_DIM // 2, 1) * sin
         for h in range(tn // HEAD_DIM)], axis=1)


def _ffn(x, h, w_in, w_out, layer, tm):
    tf = 256
    d_ff = w_out.shape[1]
    assert d_ff % tf == 0
    g, w_out_bf16 = _matmul([h], [(w_in, layer, 0, 0), (w_in, layer, 0, d_ff // tf)], [], _swiglu_epilogue,
                            d_ff, BF16, tm, tf, "ffn_in", n_split=2, side=(w_out, layer))
    tn = 512
    return _matmul([g], [(w_out_bf16, 0, 0, 0)], [(x, (tm // 2, tn), lambda j, i: (i, j))],
                   functools.partial(_residual_epilogue, scale=0.5), x.shape[1], F32, tm // 2, tn, "ffn_out")


def _softmax_pv(s, valid, v, sink=None):
    s = jnp.where(valid, s, NEG)
    m = jnp.max(s, axis=-1, keepdims=True)
    if sink is not None:
        m = jnp.maximum(m, sink)
    e = jnp.exp(s - m)
    l = jnp.sum(e, axis=-1, keepdims=True)
    denom = l if sink is None else l + jnp.exp(sink - m)
    return _dot(e.astype(BF16), v) / denom, m + jnp.log(l)


def _swa_prompt_kernel(q_ref, kc_ref, kp_ref, vc_ref, vp_ref, sink_ref, o_ref, *, n_kv, grp, max_dist):
    assert max_dist < BLOCK
    rows = grp * BLOCK
    scale = HEAD_DIM ** -0.5
    prev_on = jnp.where(pl.program_id(1) == 0, 0, 1)
    qi = lax.broadcasted_iota(jnp.int32, (rows, BLOCK), 0) & (BLOCK - 1)
    kj = lax.broadcasted_iota(jnp.int32, (rows, BLOCK), 1)
    tri = kj <= qi
    valid = (jnp.where(tri, qi - kj, qi + BLOCK - kj) <= max_dist) & (kj <= qi + prev_on * BLOCK)

    for kh in range(n_kv):
        q = jnp.concatenate([q_ref[:, _head(kh * grp + g)] for g in range(grp)], axis=0).astype(BF16)
        s_cur = _dot_nt(q, kc_ref[:, _head(kh)].astype(BF16))
        s_prev = _dot_nt(q, kp_ref[:, _head(kh)].astype(BF16))
        s = jnp.where(valid, jnp.where(tri, s_cur, s_prev) * scale, NEG)
        sink = jnp.concatenate(
            [jnp.full((BLOCK, 1), sink_ref[kh * grp + g], F32) for g in range(grp)], axis=0)
        m = jnp.maximum(jnp.max(s, axis=-1, keepdims=True), sink)
        e = jnp.exp(s - m)
        denom = jnp.sum(e, axis=-1, keepdims=True) + jnp.exp(sink - m)
        o = (_dot(jnp.where(tri, e, 0.0).astype(BF16), vc_ref[:, _head(kh)].astype(BF16))
             + _dot(jnp.where(tri, 0.0, e).astype(BF16), vp_ref[:, _head(kh)].astype(BF16))) / denom
        for g in range(grp):
            o_ref[:, _head(kh * grp + g)] = o[g * BLOCK:(g + 1) * BLOCK].astype(o_ref.dtype)


def _swa_prompt(qkv, n_batch, seq, n_kv, grp, max_dist, sink):
    wq, wk = n_kv * grp * HEAD_DIM, n_kv * HEAD_DIM
    assert seq % BLOCK == 0 and wq % wk == 0
    nbl = seq // BLOCK
    k_blk, v_blk = wq // wk, wq // wk + 1

    def imap(col, prev):
        return lambda n, b: (n * nbl + (jnp.maximum(b - 1, 0) if prev else b), col)

    return pl.pallas_call(
        functools.partial(_swa_prompt_kernel, n_kv=n_kv, grp=grp, max_dist=max_dist),
        grid=(n_batch, nbl),
        in_specs=[pl.BlockSpec((BLOCK, wq), imap(0, False)),
                  pl.BlockSpec((BLOCK, wk), imap(k_blk, False)), pl.BlockSpec((BLOCK, wk), imap(k_blk, True)),
                  pl.BlockSpec((BLOCK, wk), imap(v_blk, False)), pl.BlockSpec((BLOCK, wk), imap(v_blk, True)),
                  pl.BlockSpec(memory_space=pltpu.SMEM)],
        out_specs=pl.BlockSpec((BLOCK, wq), lambda n, b: (n * nbl + b, 0)),
        out_shape=jax.ShapeDtypeStruct((n_batch * seq, wq), BF16),
        compiler_params=_params(2),
        name="swa_prompt",
    )(qkv, qkv, qkv, qkv, qkv, sink)


def _dilated_prompt_kernel(*refs, groups, seq):
    n_g = len(groups)
    qkv_refs = refs[:3 * n_g]
    o_ref = refs[3 * n_g]
    o_scr = refs[3 * n_g + 1:3 * n_g + 1 + n_g]
    l_scr = refs[3 * n_g + 1 + n_g:]
    scale = HEAD_DIM ** -0.5

    def band(n_keys, off, max_dist):
        qi = lax.broadcasted_iota(jnp.int32, (BLOCK, n_keys), 0)
        kj = lax.broadcasted_iota(jnp.int32, (BLOCK, n_keys), 1)
        dist = qi + off - kj
        return (dist >= 0) & (dist <= max_dist)

    for g, (max_dist, dil) in enumerate(groups):
        q_ref, k_ref, v_ref = qkv_refs[3 * g:3 * g + 3]
        valid_first, valid_next = band(BLOCK, 0, max_dist), band(2 * BLOCK, BLOCK, max_dist)

        def rows(start, dil=dil):
            return pl.ds(start, BLOCK, stride=dil) if dil > 1 else pl.ds(start, BLOCK)

        units = [(r, b) for r in range(dil) for b in range(seq // dil // BLOCK)]
        for u0 in range(0, len(units), UNITS_IN_FLIGHT):
            staged = []
            for r, b in units[u0:u0 + UNITS_IN_FLIGHT]:
                cur = rows(r + b * BLOCK * dil)
                q = q_ref[cur, :].astype(BF16)
                if b == 0:
                    k, v, valid = k_ref[cur, :], v_ref[cur, :], valid_first
                else:
                    prev = rows(r + (b - 1) * BLOCK * dil)
                    k = jnp.concatenate([k_ref[prev, :], k_ref[cur, :]], axis=0)
                    v = jnp.concatenate([v_ref[prev, :], v_ref[cur, :]], axis=0)
                    valid = valid_next
                staged.append((cur, jnp.where(valid, _dot_nt(q, k.astype(BF16)) * scale, NEG), v.astype(BF16)))
            probs = []
            for cur, s, v in staged:
                m = jnp.max(s, axis=-1, keepdims=True)
                e = jnp.exp(s - m)
                l = jnp.sum(e, axis=-1, keepdims=True)
                probs.append((cur, e.astype(BF16), v, l, m))
            for cur, e, v, l, m in probs:
                o_scr[g][cur, :] = _dot(e, v) / l
                l_scr[g][cur, :] = jnp.broadcast_to(m + jnp.log(l), (BLOCK, HEAD_DIM))

    chunk = 2 * BLOCK
    for c in range(seq // chunk):
        cs = pl.ds(c * chunk, chunk)
        lses = [l[cs, :] for l in l_scr]
        m = functools.reduce(jnp.maximum, lses)
        ws = [jnp.exp(l - m) for l in lses]
        tot = functools.reduce(jnp.add, ws)
        mix = functools.reduce(jnp.add, [(w / tot) * o[cs, :] for w, o in zip(ws, o_scr)])
        o_ref[cs, :] = mix.astype(o_ref.dtype)


def _dilated_prompt(qkv, n_batch, seq, col0, n_heads, pairs):
    groups = tuple((window // dil, dil) for window, dil in pairs)
    assert all(seq % (dil * BLOCK) == 0 and window // dil <= BLOCK for window, dil in pairs)
    w = n_heads * HEAD_DIM
    in_specs = []
    for c in col0:
        assert c % HEAD_DIM == 0
        for part in range(3):
            cb = (c + part * w) // HEAD_DIM
            in_specs.append(pl.BlockSpec((seq, HEAD_DIM), lambda n, h, cb=cb: (n, cb + h)))
    n_g = len(groups)
    return pl.pallas_call(
        functools.partial(_dilated_prompt_kernel, groups=groups, seq=seq),
        grid=(n_batch, n_heads),
        in_specs=in_specs,
        out_specs=pl.BlockSpec((seq, HEAD_DIM), lambda n, h: (n, h)),
        out_shape=jax.ShapeDtypeStruct((n_batch * seq, w), BF16),
        scratch_shapes=[pltpu.VMEM((seq, HEAD_DIM), F32) for _ in range(2 * n_g)],
        compiler_params=_params(2),
        name="dilated_prompt",
    )(*([qkv] * (3 * n_g)))


def _export_kernel(k_ref, v_ref, prev_ref, o_ref, *, n_kv):
    del prev_ref
    for h in range(n_kv):
        o_ref[:, 0, h, :] = k_ref[:, _head(h)]
        o_ref[:, 1, h, :] = v_ref[:, _head(h)]


def _export_window(qkv, prev, layer, depth, n_batch, seq, lw, k_col, n_kv):
    wk = n_kv * HEAD_DIM
    rb = min(lw, 256)
    assert lw % rb == 0 and seq % rb == 0 and k_col % wk == 0
    shape = (depth, n_batch, lw, 2, n_kv, HEAD_DIM)
    first = prev is None
    if first:
        prev = jnp.zeros((8, HEAD_DIM), F32)
    row_blk0 = (seq - lw) // rb

    def imap(part):
        return lambda n, c: (n * (seq // rb) + row_blk0 + c, k_col // wk + part)

    return pl.pallas_call(
        functools.partial(_export_kernel, n_kv=n_kv),
        grid=(n_batch, lw // rb),
        in_specs=[pl.BlockSpec((rb, wk), imap(0)), pl.BlockSpec((rb, wk), imap(1)),
                  pl.BlockSpec(memory_space=pl.ANY)],
        out_specs=pl.BlockSpec((None, None, rb, 2, n_kv, HEAD_DIM), lambda n, c: (layer, n, c, 0, 0, 0)),
        out_shape=jax.ShapeDtypeStruct(shape, F32),
        input_output_aliases={} if first else {2: 0},
        compiler_params=_params(2),
        name="export_window",
    )(qkv, qkv, prev)


def _sample_group(q_ref, q_col, new_ref, k_col, c_ref, *, n_kv, grp, dil, lb, max_dist, sink, t_new):
    n_q = n_kv * grp
    wk = n_kv * HEAD_DIM
    rows = n_q * t_new
    n_cache = lb // dil
    n_keys = n_cache + BLOCK
    scale = HEAD_DIM ** -0.5
    assert dil & (dil - 1) == 0 and t_new & (t_new - 1) == 0

    def lane_block(x, idx, n):
        zero = jnp.zeros_like(x)
        return jnp.concatenate([x if i == idx else zero for i in range(n)], axis=1)

    q_bd = jnp.concatenate(
        [lane_block(q_ref[:, q_col + h * HEAD_DIM:q_col + (h + 1) * HEAD_DIM], h // grp, n_kv)
         for h in range(n_q)], axis=0).astype(BF16)
    pad = jnp.zeros((BLOCK - t_new, wk), F32)
    k_new = jnp.concatenate([new_ref[:, k_col:k_col + wk], pad], axis=0)
    v_new = jnp.concatenate([new_ref[:, k_col + wk:k_col + 2 * wk], pad], axis=0)

    t_row = lax.broadcasted_iota(jnp.int32, (rows, n_keys), 0) & (t_new - 1)
    lane = lax.broadcasted_iota(jnp.int32, (rows, n_keys), 1)
    t_col = lax.broadcasted_iota(jnp.int32, (rows, 1), 0) & (t_new - 1)

    o_all = jnp.zeros((rows, wk), F32)
    lse_all = jnp.zeros((rows, 1), F32)
    for r in range(min(dil, t_new)):
        k_c = jnp.concatenate([c_ref[:, 2 * r, h, :] for h in range(n_kv)], axis=1)
        v_c = jnp.concatenate([c_ref[:, 2 * r + 1, h, :] for h in range(n_kv)], axis=1)
        k = jnp.concatenate([k_c, k_new], axis=0).astype(BF16)
        v = jnp.concatenate([v_c, v_new], axis=0).astype(BF16)
        delta = jnp.where(lane < n_cache, lb + t_row - (r + dil * lane), t_row - (lane - n_cache))
        valid = ((delta >= 0) & (delta <= max_dist) & ((delta & (dil - 1)) == 0)
                 & (lane < n_cache + t_new))
        o, lse = _softmax_pv(_dot_nt(q_bd, k) * scale, valid, v, sink)
        mine = ((t_col + lb) & (dil - 1)) == r
        o_all = jnp.where(mine, o, o_all)
        lse_all = jnp.where(mine, lse, lse_all)
    return o_all, lse_all


def _sample_dilated_group(q_ref, new_ref, c_ref, *, n_heads, dil, lb, max_dist, t_new):
    assert n_heads == 8 and dil & (dil - 1) == 0
    h_bits = n_heads.bit_length() - 1
    n_cache = lb // dil
    n_c, n_n = n_cache * n_heads, t_new * n_heads
    scale = HEAD_DIM ** -0.5
    q_all = q_ref[...].reshape(n_n, HEAD_DIM)
    k_new = new_ref[:, 0, :, :].reshape(n_n, HEAD_DIM).astype(BF16)
    v_new = new_ref[:, 1, :, :].reshape(n_n, HEAD_DIM).astype(BF16)
    outs, lses = [None] * t_new, [None] * t_new
    scores = []
    for r in range(min(dil, t_new)):
        ts = [t for t in range(t_new) if (lb + t) % dil == r]
        rows = n_heads * len(ts)
        q = jnp.concatenate([q_all[t * n_heads:(t + 1) * n_heads] for t in ts], axis=0).astype(BF16)
        k_c = c_ref[:, 2 * r, :, :].reshape(n_c, HEAD_DIM).astype(BF16)

        def mask(n_cols, key_row, rows=rows, ts=ts):
            row = lax.broadcasted_iota(jnp.int32, (rows, n_cols), 0)
            col = lax.broadcasted_iota(jnp.int32, (rows, n_cols), 1)
            t_q = ts[0] + dil * (row >> h_bits)
            delta = lb + t_q - key_row(col >> h_bits)
            same_head = (row & (n_heads - 1)) == (col & (n_heads - 1))
            return same_head & (delta >= 0) & (delta <= max_dist) & ((delta & (dil - 1)) == 0)

        s_c = jnp.where(mask(n_c, lambda k, r=r: r + dil * k), _dot_nt(q, k_c) * scale, NEG)
        s_n = jnp.where(mask(n_n, lambda t: lb + t), _dot_nt(q, k_new) * scale, NEG)
        scores.append((r, ts, s_c, s_n))
    probs = []
    for r, ts, s_c, s_n in scores:
        m = jnp.maximum(jnp.max(s_c, axis=-1, keepdims=True), jnp.max(s_n, axis=-1, keepdims=True))
        e_c, e_n = jnp.exp(s_c - m), jnp.exp(s_n - m)
        l = jnp.sum(e_c, axis=-1, keepdims=True) + jnp.sum(e_n, axis=-1, keepdims=True)
        probs.append((r, ts, e_c.astype(BF16), e_n.astype(BF16), l, m + jnp.log(l)))
    for r, ts, e_c, e_n, l, lse in probs:
        v_c = c_ref[:, 2 * r + 1, :, :].reshape(n_c, HEAD_DIM).astype(BF16)
        o = (_dot(e_c, v_c) + _dot(e_n, v_new)) / l
        for i, t in enumerate(ts):
            outs[t] = o[i * n_heads:(i + 1) * n_heads]
            lses[t] = lse[i * n_heads:(i + 1) * n_heads]
    return jnp.concatenate(outs, axis=0), jnp.concatenate(lses, axis=0)


def _sample_kernel(*refs, cfg):
    t_new, n_kv_a, grp_a, n_heads_b, pairs, lbs, lb_a, col_b = (
        cfg["t_new"], cfg["n_kv_a"], cfg["grp_a"], cfg["n_heads_b"], cfg["pairs"], cfg["lbs"],
        cfg["lb_a"], cfg["col_b"])
    n_g = len(pairs)
    qkv_ref, ca_ref = refs[0], refs[1]
    cb_refs = refs[2:2 + n_g]
    sink_ref = refs[2 + n_g]
    pos = 3 + n_g + (1 + n_g)
    oa_ref, ob_ref, na_ref = refs[pos], refs[pos + 1], refs[pos + 2]
    nb_refs = refs[pos + 3:pos + 3 + n_g]
    q_scrs = refs[pos + 3 + n_g:pos + 3 + 2 * n_g]
    mix_scr = refs[pos + 3 + 2 * n_g]

    n_q_a = n_kv_a * grp_a
    w_qa, w_ka, w_b = n_q_a * HEAD_DIM, n_kv_a * HEAD_DIM, n_heads_b * HEAD_DIM

    sink = jnp.concatenate([jnp.full((t_new, 1), sink_ref[h], F32) for h in range(n_q_a)], axis=0)
    o, _ = _sample_group(qkv_ref, 0, qkv_ref, w_qa, ca_ref, n_kv=n_kv_a, grp=grp_a, dil=1, lb=lb_a,
                         max_dist=WINDOW_A - 1, sink=sink, t_new=t_new)
    for h in range(n_q_a):
        oa_ref[:, _head(h)] = o[h * t_new:(h + 1) * t_new, _head(h // grp_a)].astype(oa_ref.dtype)
    for h in range(n_kv_a):
        na_ref[:, 0, h, :] = qkv_ref[:, w_qa + h * HEAD_DIM:w_qa + (h + 1) * HEAD_DIM]
        na_ref[:, 1, h, :] = qkv_ref[:, w_qa + w_ka + h * HEAD_DIM:w_qa + w_ka + (h + 1) * HEAD_DIM]

    outs, lses = [], []
    for g, (window, dil) in enumerate(pairs):
        c0 = col_b[g]
        for h in range(n_heads_b):
            nb_refs[g][:, 0, h, :] = qkv_ref[:, c0 + w_b + h * HEAD_DIM:c0 + w_b + (h + 1) * HEAD_DIM]
            nb_refs[g][:, 1, h, :] = qkv_ref[:, c0 + 2 * w_b + h * HEAD_DIM:c0 + 2 * w_b + (h + 1) * HEAD_DIM]
            q_scrs[g][:, h, :] = qkv_ref[:, c0 + h * HEAD_DIM:c0 + (h + 1) * HEAD_DIM]
        o, lse = _sample_dilated_group(q_scrs[g], nb_refs[g], cb_refs[g], n_heads=n_heads_b, dil=dil,
                                       lb=lbs[g], max_dist=window, t_new=t_new)
        outs.append(o)
        lses.append(lse)
    m = functools.reduce(jnp.maximum, lses)
    ws = [jnp.exp(l - m) for l in lses]
    tot = functools.reduce(jnp.add, ws)
    mix = functools.reduce(jnp.add, [(w / tot) * o for w, o in zip(ws, outs)])
    mix_scr[...] = mix.reshape(t_new, n_heads_b, HEAD_DIM)
    for h in range(n_heads_b):
        ob_ref[:, _head(h)] = mix_scr[:, h, :].astype(ob_ref.dtype)


def _sample_step(qkv, row0, cache_a, caches_b, sink, prev_new, layer, pairs, col_b, t_new):
    depth, n_req, lb_a, _, n_kv_a, _ = cache_a.shape
    n_q_a = sink.shape[0]
    n_heads_b = caches_b[0].shape[4]
    in_width = qkv.shape[1]
    assert row0 % t_new == 0 and t_new % 8 == 0
    lbs = tuple(c.shape[2] for c in caches_b)
    cfg = dict(t_new=t_new, n_kv_a=n_kv_a, grp_a=n_q_a // n_kv_a, n_heads_b=n_heads_b, pairs=pairs,
               lbs=lbs, lb_a=lb_a, col_b=col_b)

    def cache_spec(cache, lb, dil):
        n_kv = cache.shape[4]
        assert lb % dil == 0 and lb // dil == BLOCK
        view = cache.reshape(depth, n_req, lb // dil, dil * 2, n_kv, HEAD_DIM)
        n_cls = min(dil, t_new)
        return view, pl.BlockSpec((None, None, lb // dil, n_cls * 2, n_kv, HEAD_DIM),
                                  lambda b: (layer, b, 0, 0, 0, 0))

    args = [qkv]
    in_specs = [pl.BlockSpec((t_new, in_width), lambda b: (row0 // t_new + b, 0))]
    view, spec = cache_spec(cache_a, lb_a, 1)
    args.append(view)
    in_specs.append(spec)
    for cache, lb, (window, dil) in zip(caches_b, lbs, pairs):
        view, spec = cache_spec(cache, lb, dil)
        args.append(view)
        in_specs.append(spec)
    args.append(sink)
    in_specs.append(pl.BlockSpec(memory_space=pltpu.SMEM))

    new_shapes = [(depth, n_req, t_new, 2, n_kv_a, HEAD_DIM)] + [(depth, n_req, t_new, 2, n_heads_b, HEAD_DIM)] * len(pairs)
    first = prev_new is None
    if first:
        prev_new = [jnp.zeros((8, HEAD_DIM), F32) for _ in new_shapes]
    n_in = len(args)
    args += list(prev_new)
    in_specs += [pl.BlockSpec(memory_space=pl.ANY)] * len(prev_new)

    n_s = n_req * t_new
    out_shape = [jax.ShapeDtypeStruct((n_s, n_q_a * HEAD_DIM), BF16),
                 jax.ShapeDtypeStruct((n_s, n_heads_b * HEAD_DIM), BF16)]
    out_specs = [pl.BlockSpec((t_new, n_q_a * HEAD_DIM), lambda b: (b, 0)),
                 pl.BlockSpec((t_new, n_heads_b * HEAD_DIM), lambda b: (b, 0))]
    for s in new_shapes:
        out_shape.append(jax.ShapeDtypeStruct(s, F32))
        out_specs.append(pl.BlockSpec((None, None) + s[2:], lambda b: (layer, b, 0, 0, 0, 0)))
    aliases = {} if first else {n_in + i: 2 + i for i in range(len(new_shapes))}
    outs = pl.pallas_call(
        functools.partial(_sample_kernel, cfg=cfg),
        grid=(n_req,),
        in_specs=in_specs,
        out_specs=out_specs,
        out_shape=out_shape,
        input_output_aliases=aliases,
        scratch_shapes=[pltpu.VMEM((t_new, n_heads_b, HEAD_DIM), F32) for _ in range(len(pairs) + 1)],
        compiler_params=_params(1),
        name="sample_step",
    )(*args)
    return outs[0], outs[1], list(outs[2:])


def _rope_tables(positions):
    half = HEAD_DIM // 2
    inv = ROPE_THETA ** (-jnp.arange(half, dtype=F32) / half)
    ang = positions.astype(F32)[:, None] * inv[None, :]
    cos, sin = jnp.cos(ang), jnp.sin(ang)
    return jnp.concatenate([cos, cos], axis=1), jnp.concatenate([-sin, sin], axis=1)


def kernel(x_prompt, x_sample, cache_a_kv, cache_b1_kv, cache_b2_kv, cache_b3_kv, norm_ffn1, w_ffn1_in,
           w_ffn1_out, norm_mix, w_in, sinks, w_gate, w_up_a, w_up_b, w_o, norm_ffn2, w_ffn2_in,
           w_ffn2_out, norm_final):
    batch, seq, d_model = x_prompt.shape
    n_req, t_new, _ = x_sample.shape
    depth = w_in.shape[0]
    n_heads_a = sinks.shape[1]
    n_kv_a = cache_a_kv.shape[4]
    grp_a = n_heads_a // n_kv_a
    n_heads_b = cache_b1_kv.shape[4]
    w_qa, w_ka, w_b = n_heads_a * HEAD_DIM, n_kv_a * HEAD_DIM, n_heads_b * HEAD_DIM
    in_width = w_in.shape[2]
    assert in_width == w_qa + 2 * w_ka + 9 * w_b
    caches_b = (cache_b1_kv, cache_b2_kv, cache_b3_kv)

    n_p, n_s = batch * seq, n_req * t_new
    n_tok = n_p + n_s
    tm = n_tok // 8
    assert n_tok % (8 * 32) == 0

    x = jnp.concatenate([x_prompt.reshape(n_p, d_model), x_sample.reshape(n_s, d_model)], axis=0)
    positions = jnp.concatenate([jnp.tile(jnp.arange(seq), batch),
                                 jnp.tile(PAST_LEN + jnp.arange(t_new), n_req)])
    cos_t, sin_t = _rope_tables(positions)

    tn_qkv = 512
    seg = [(w_qa, True), (w_ka, True), (w_ka, False)] + [(w_b, True), (w_b, True), (w_b, False)] * 3
    rope_flags = tuple(int(f) for w, f in seg for _ in range(w // tn_qkv))
    assert len(rope_flags) == in_width // tn_qkv
    col_b = tuple(w_qa + 2 * w_ka + 3 * w_b * g for g in range(3))

    new_p = [None] * (1 + len(DILATED_PAIRS))
    new_s = None
    tm_norm = n_s
    for l in range(depth):
        h = _rmsnorm(x, norm_ffn1[l], BF16, tm_norm)
        x = _ffn(x, h, w_ffn1_in, w_ffn1_out, l, tm)

        h = _rmsnorm(x, norm_mix[l], BF16, tm_norm)
        qkv = _matmul([h], [(w_in, l, 0, 0)],
                      [(cos_t, (tm, HEAD_DIM), lambda j, i: (i, 0)), (sin_t, (tm, HEAD_DIM), lambda j, i: (i, 0))],
                      functools.partial(_rope_epilogue, tn=tn_qkv, rope_flags=rope_flags),
                      in_width, F32, tm, tn_qkv, "qkv_rope", n_split=2)

        oa_p = _swa_prompt(qkv, batch, seq, n_kv_a, grp_a, WINDOW_A - 1, sinks[l])
        ob_p = _dilated_prompt(qkv, batch, seq, col_b, n_heads_b, DILATED_PAIRS)
        oa_s, ob_s, new_s = _sample_step(qkv, n_p, cache_a_kv, caches_b, sinks[l], new_s, l,
                                         DILATED_PAIRS, col_b, t_new)
        new_p[0] = _export_window(qkv, new_p[0], l, depth, batch, seq, min(WINDOW_A, seq), w_qa, n_kv_a)
        for g, (window, dil) in enumerate(DILATED_PAIRS):
            new_p[1 + g] = _export_window(qkv, new_p[1 + g], l, depth, batch, seq, min(window, seq),
                                          col_b[g] + w_b, n_heads_b)
        o_a = jnp.concatenate([oa_p, oa_s], axis=0)
        o_b = jnp.concatenate([ob_p, ob_s], axis=0)

        tn_g = 256
        u = _matmul([h, o_a, o_b],
                    [(w_gate, l, 0, 0), (w_gate, l, 0, d_model // tn_g), (w_up_a, l, 1, 0), (w_up_b, l, 2, 0)],
                    [], _gate_epilogue, d_model, BF16, tm // 2, tn_g, "gate_up")
        x = _matmul([u], [(w_o, l, 0, 0)], [(x, (tm, 512), lambda j, i: (i, j))],
                    functools.partial(_residual_epilogue, scale=1.0), d_model, F32, tm, 512, "w_o", n_split=2)

        h = _rmsnorm(x, norm_ffn2[l], BF16, tm_norm)
        x = _ffn(x, h, w_ffn2_in, w_ffn2_out, l, tm)

    y_prompt = _rmsnorm(x, norm_final, F32, tm_norm, 0, n_p).reshape(batch, seq, d_model)
    y_sample = _rmsnorm(x, norm_final, F32, tm_norm, n_p, n_s).reshape(n_req, t_new, d_model)
    return (y_prompt, y_sample, new_p[0], new_s[0], new_p[1], new_s[1], new_p[2], new_s[2], new_p[3], new_s[3])
```

```python
import functools

import jax
import jax.numpy as jnp
from jax import lax
from jax.experimental import pallas as pl
from jax.experimental.pallas import tpu as pltpu

HEAD_DIM = 128
BLOCK = 128
WINDOW_A = 128
PAST_LEN = 16384
DILATED_PAIRS = ((128, 1), (512, 4), (2048, 16))
ROPE_THETA = 10000.0
EPS = 1e-5
NEG = -1e30
VMEM_LIMIT = 56 * 1024 * 1024
UNITS_IN_FLIGHT = 8

F32 = jnp.float32
BF16 = jnp.bfloat16


def _params(n_axes, vmem=VMEM_LIMIT):
    return pltpu.CompilerParams(dimension_semantics=("arbitrary",) * n_axes, vmem_limit_bytes=vmem)


def _dot(a, b):
    return jnp.dot(a, b, preferred_element_type=F32)


def _dot_nt(a, b):
    return lax.dot_general(a, b, (((1,), (1,)), ((), ())), preferred_element_type=F32)


def _head(h):
    return slice(h * HEAD_DIM, (h + 1) * HEAD_DIM)


def _rmsnorm_kernel(x_ref, g_ref, o_ref):
    x = x_ref[...]
    y = x * lax.rsqrt(jnp.mean(x * x, axis=-1, keepdims=True) + EPS)
    o_ref[...] = (y * g_ref[...]).astype(o_ref.dtype)


def _rmsnorm(x, g, out_dtype, tm, row0=0, rows=None):
    d = x.shape[1]
    rows = x.shape[0] if rows is None else rows
    assert rows % tm == 0 and row0 % tm == 0
    blk0 = row0 // tm
    return pl.pallas_call(
        _rmsnorm_kernel,
        grid=(rows // tm,),
        in_specs=[pl.BlockSpec((tm, d), lambda i: (blk0 + i, 0)), pl.BlockSpec((1, d), lambda i: (0, 0))],
        out_specs=pl.BlockSpec((tm, d), lambda i: (i, 0)),
        out_shape=jax.ShapeDtypeStruct((rows, d), out_dtype),
        compiler_params=_params(1),
        name="rmsnorm",
    )(x, g.reshape(1, d))


def _mm_kernel(*refs, n_a, b_to_a, b_is_f32, n_extra, has_side, epilogue, n_split, lead):
    n_b = len(b_to_a)
    a_refs = refs[:n_a]
    b_refs = refs[n_a:n_a + n_b]
    pos = n_a + n_b
    extra_refs = refs[pos:pos + n_extra]
    pos += n_extra
    side_in = refs[pos] if has_side else None
    pos += int(has_side)
    o_ref = refs[pos]
    side_out = refs[pos + 1] if has_side else None
    scratch = list(refs[pos + 1 + int(has_side):])
    w_refs = [scratch.pop(0) if f32 else b_ref for b_ref, f32 in zip(b_refs, b_is_f32)]
    jj, i = pl.program_id(0), pl.program_id(1)

    if has_side:
        side_out[...] = side_in[...].astype(BF16)

    def fill():
        for b_ref, w_ref, f32 in zip(b_refs, w_refs, b_is_f32):
            if f32:
                kc = b_ref.shape[0]
                w_ref[jj % 2, pl.ds(pl.multiple_of(i * kc, kc), kc), :] = b_ref[...].astype(BF16)

    def compute():
        ws = [w_ref[(jj + 1) % 2] if f32 else w_ref[...] for w_ref, f32 in zip(w_refs, b_is_f32)]
        rows_per = o_ref.shape[0] // n_split
        for s in range(n_split):
            rows = pl.ds(s * rows_per, rows_per)
            accs = [_dot(a_refs[ai][rows, :], ws[k]) for k, ai in enumerate(b_to_a)]
            out = epilogue(accs, [e[rows, :] for e in extra_refs], jj - lead)
            o_ref[rows, :] = out.astype(o_ref.dtype)

    if lead:
        pl.when(jj == 0)(fill)

        @pl.when(jj > 0)
        def _():
            fill()
            compute()
    else:
        compute()


def _matmul(a_list, b_list, extras, epilogue, n_out, out_dtype, tm, tn, name, n_split=1, side=None):
    m = a_list[0].shape[0]
    n_j, n_i = n_out // tn, m // tm
    assert m % tm == 0 and n_out % tn == 0 and tm % (16 * n_split) == 0
    lead = int(any(b[0].ndim == 3 for b in b_list))

    def col(jj):
        return jnp.maximum(jj - lead, 0)

    def row(jj, i):
        return jnp.where(jj < lead, 0, i)

    in_specs = [pl.BlockSpec((tm, a.shape[1]), lambda jj, i: (row(jj, i), 0)) for a in a_list]
    args = list(a_list)
    scratch = []
    for w, layer, ai, off in b_list:
        k = a_list[ai].shape[1]
        if w.ndim == 3:
            assert w.shape[1] == k and w.shape[2] % tn == 0 and k % (16 * n_i) == 0
            in_specs.append(pl.BlockSpec(
                (None, k // n_i, tn),
                lambda jj, i, layer=layer, off=off: (layer, i, jnp.minimum(jj, n_j - 1) + off)))
            args.append(w)
            scratch.append(pltpu.VMEM((2, k, tn), BF16))
        else:
            assert w.dtype == BF16 and w.shape[0] == k and w.shape[1] % tn == 0
            in_specs.append(pl.BlockSpec((k, tn), lambda jj, i, off=off: (0, col(jj) + off)))
            args.append(w)
    for arr, shape, imap in extras:
        assert shape[0] == tm
        in_specs.append(pl.BlockSpec(shape, lambda jj, i, imap=imap: imap(col(jj), row(jj, i))))
        args.append(arr)
    out_shape = [jax.ShapeDtypeStruct((m, n_out), out_dtype)]
    out_specs = [pl.BlockSpec((tm, tn), lambda jj, i: (row(jj, i), col(jj)))]
    if side is not None:
        w_side, layer = side
        _, r, c = w_side.shape
        rs = r // (n_j * n_i)
        assert r % (n_j * n_i) == 0 and rs % 16 == 0
        in_specs.append(pl.BlockSpec((None, rs, c), lambda jj, i: (layer, col(jj) * n_i + row(jj, i), 0)))
        out_shape.append(jax.ShapeDtypeStruct((r, c), BF16))
        out_specs.append(pl.BlockSpec((rs, c), lambda jj, i: (col(jj) * n_i + row(jj, i), 0)))
        args.append(w_side)
    kernel = functools.partial(_mm_kernel, n_a=len(a_list), b_to_a=tuple(b[2] for b in b_list),
                               b_is_f32=tuple(b[0].ndim == 3 for b in b_list),
                               n_extra=len(extras), has_side=side is not None, epilogue=epilogue,
                               n_split=n_split, lead=lead)
    outs = pl.pallas_call(
        kernel,
        grid=(n_j + lead, m // tm),
        in_specs=in_specs,
        out_specs=out_specs,
        out_shape=out_shape,
        scratch_shapes=scratch,
        compiler_params=_params(2),
        name=name,
    )(*args)
    return outs[0] if side is None else outs


def _swiglu_epilogue(accs, extras, j):
    a, b = accs
    return a * jax.nn.sigmoid(a) * b


def _residual_epilogue(accs, extras, j, *, scale):
    return extras[0] + scale * accs[0]


def _gate_epilogue(accs, extras, j):
    g_a, g_b, u_a, u_b = accs
    return jax.nn.sigmoid(g_a) * u_a + jax.nn.sigmoid(g_b) * u_b


def _rope_epilogue(accs, extras, j, *, tn, rope_flags):
    acc = accs[0]
    cos, sin = extras
    per_tile = tn // HEAD_DIM
    n_tiles = len(rope_flags) // per_tile
    assert n_tiles < 31 and len(rope_flags) == n_tiles * per_tile
    heads = []
    for h in range(per_tile):
        flag_bits = sum(int(rope_flags[t * per_tile + h]) << t for t in range(n_tiles))
        bit = lax.shift_right_logical(jnp.int32(flag_bits), j) & 1
        on = jnp.broadcast_to(bit, (1, HEAD_DIM)).astype(F32)
        x = acc[:, _head(h)]
        heads.append(x * (on * cos + (1.0 - on)) + pltpu.roll(x, HEAD_DIM // 2, 1) * (on * sin))
    return jnp.concatenate(heads, axis=1)


def _ffn(x, h, w_in, w_out, layer, tm):
    tf = 256
    d_ff = w_out.shape[1]
    assert d_ff % tf == 0
    g, w_out_bf16 = _matmul([h], [(w_in, layer, 0, 0), (w_in, layer, 0, d_ff // tf)], [], _swiglu_epilogue,
                            d_ff, BF16, 2 * tm, tf, "ffn_in", n_split=2, side=(w_out, layer))
    tn = 512
    return _matmul([g], [(w_out_bf16, 0, 0, 0)], [(x, (tm // 2, tn), lambda j, i: (i, j))],
                   functools.partial(_residual_epilogue, scale=0.5), x.shape[1], F32, tm // 2, tn, "ffn_out")


def _softmax_pv(s, valid, v, sink=None):
    s = jnp.where(valid, s, NEG)
    m = jnp.max(s, axis=-1, keepdims=True)
    if sink is not None:
        m = jnp.maximum(m, sink)
    e = jnp.exp(s - m)
    l = jnp.sum(e, axis=-1, keepdims=True)
    denom = l if sink is None else l + jnp.exp(sink - m)
    return _dot(e.astype(BF16), v) / denom, m + jnp.log(l)


def _swa_prompt_kernel(q_ref, kc_ref, kp_ref, vc_ref, vp_ref, sink_ref, o_ref, *, n_kv, grp, max_dist):
    assert max_dist < BLOCK
    rows = grp * BLOCK
    scale = HEAD_DIM ** -0.5
    prev_on = jnp.where(pl.program_id(1) == 0, 0, 1)
    qi = lax.broadcasted_iota(jnp.int32, (rows, BLOCK), 0) & (BLOCK - 1)
    kj = lax.broadcasted_iota(jnp.int32, (rows, BLOCK), 1)
    tri = kj <= qi
    valid = (jnp.where(tri, qi - kj, qi + BLOCK - kj) <= max_dist) & (kj <= qi + prev_on * BLOCK)

    for kh in range(n_kv):
        q = jnp.concatenate([q_ref[:, _head(kh * grp + g)] for g in range(grp)], axis=0).astype(BF16)
        s_cur = _dot_nt(q, kc_ref[:, _head(kh)].astype(BF16))
        s_prev = _dot_nt(q, kp_ref[:, _head(kh)].astype(BF16))
        s = jnp.where(valid, jnp.where(tri, s_cur, s_prev) * scale, NEG)
        sink = jnp.concatenate(
            [jnp.full((BLOCK, 1), sink_ref[kh * grp + g], F32) for g in range(grp)], axis=0)
        m = jnp.maximum(jnp.max(s, axis=-1, keepdims=True), sink)
        e = jnp.exp(s - m)
        denom = jnp.sum(e, axis=-1, keepdims=True) + jnp.exp(sink - m)
        o = (_dot(jnp.where(tri, e, 0.0).astype(BF16), vc_ref[:, _head(kh)].astype(BF16))
             + _dot(jnp.where(tri, 0.0, e).astype(BF16), vp_ref[:, _head(kh)].astype(BF16))) / denom
        for g in range(grp):
            o_ref[:, _head(kh * grp + g)] = o[g * BLOCK:(g + 1) * BLOCK].astype(o_ref.dtype)


def _swa_prompt(qkv, n_batch, seq, n_kv, grp, max_dist, sink):
    wq, wk = n_kv * grp * HEAD_DIM, n_kv * HEAD_DIM
    assert seq % BLOCK == 0 and wq % wk == 0
    nbl = seq // BLOCK
    k_blk, v_blk = wq // wk, wq // wk + 1

    def imap(col, prev):
        return lambda n, b: (n * nbl + (jnp.maximum(b - 1, 0) if prev else b), col)

    return pl.pallas_call(
        functools.partial(_swa_prompt_kernel, n_kv=n_kv, grp=grp, max_dist=max_dist),
        grid=(n_batch, nbl),
        in_specs=[pl.BlockSpec((BLOCK, wq), imap(0, False)),
                  pl.BlockSpec((BLOCK, wk), imap(k_blk, False)), pl.BlockSpec((BLOCK, wk), imap(k_blk, True)),
                  pl.BlockSpec((BLOCK, wk), imap(v_blk, False)), pl.BlockSpec((BLOCK, wk), imap(v_blk, True)),
                  pl.BlockSpec(memory_space=pltpu.SMEM)],
        out_specs=pl.BlockSpec((BLOCK, wq), lambda n, b: (n * nbl + b, 0)),
        out_shape=jax.ShapeDtypeStruct((n_batch * seq, wq), BF16),
        compiler_params=_params(2),
        name="swa_prompt",
    )(qkv, qkv, qkv, qkv, qkv, sink)


def _dilated_prompt_kernel(*refs, groups, seq):
    n_g = len(groups)
    qkv_refs = refs[:3 * n_g]
    o_ref = refs[3 * n_g]
    o_scr = refs[3 * n_g + 1:3 * n_g + 1 + n_g]
    l_scr = refs[3 * n_g + 1 + n_g:]
    scale = HEAD_DIM ** -0.5

    def band(n_keys, off, max_dist):
        qi = lax.broadcasted_iota(jnp.int32, (BLOCK, n_keys), 0)
        kj = lax.broadcasted_iota(jnp.int32, (BLOCK, n_keys), 1)
        dist = qi + off - kj
        return (dist >= 0) & (dist <= max_dist)

    for g, (max_dist, dil) in enumerate(groups):
        q_ref, k_ref, v_ref = qkv_refs[3 * g:3 * g + 3]
        valid_first, valid_next = band(BLOCK, 0, max_dist), band(2 * BLOCK, BLOCK, max_dist)

        def rows(start, dil=dil):
            return pl.ds(start, BLOCK, stride=dil) if dil > 1 else pl.ds(start, BLOCK)

        units = [(r, b) for r in range(dil) for b in range(seq // dil // BLOCK)]
        for u0 in range(0, len(units), UNITS_IN_FLIGHT):
            staged = []
            for r, b in units[u0:u0 + UNITS_IN_FLIGHT]:
                cur = rows(r + b * BLOCK * dil)
                q = q_ref[cur, :].astype(BF16)
                if b == 0:
                    k, v, valid = k_ref[cur, :], v_ref[cur, :], valid_first
                else:
                    prev = rows(r + (b - 1) * BLOCK * dil)
                    k = jnp.concatenate([k_ref[prev, :], k_ref[cur, :]], axis=0)
                    v = jnp.concatenate([v_ref[prev, :], v_ref[cur, :]], axis=0)
                    valid = valid_next
                staged.append((cur, jnp.where(valid, _dot_nt(q, k.astype(BF16)) * scale, NEG), v.astype(BF16)))
            probs = []
            for cur, s, v in staged:
                m = jnp.max(s, axis=-1, keepdims=True)
                e = jnp.exp(s - m)
                l = jnp.sum(e, axis=-1, keepdims=True)
                probs.append((cur, e.astype(BF16), v, l, m))
            for cur, e, v, l, m in probs:
                o_scr[g][cur, :] = _dot(e, v) / l
                l_scr[g][cur, :] = jnp.broadcast_to(m + jnp.log(l), (BLOCK, HEAD_DIM))

    chunk = 2 * BLOCK
    for c in range(seq // chunk):
        cs = pl.ds(c * chunk, chunk)
        lses = [l[cs, :] for l in l_scr]
        m = functools.reduce(jnp.maximum, lses)
        ws = [jnp.exp(l - m) for l in lses]
        tot = functools.reduce(jnp.add, ws)
        mix = functools.reduce(jnp.add, [(w / tot) * o[cs, :] for w, o in zip(ws, o_scr)])
        o_ref[cs, :] = mix.astype(o_ref.dtype)


def _dilated_prompt(qkv, n_batch, seq, col0, n_heads, pairs):
    groups = tuple((window // dil, dil) for window, dil in pairs)
    assert all(seq % (dil * BLOCK) == 0 and window // dil <= BLOCK for window, dil in pairs)
    w = n_heads * HEAD_DIM
    in_specs = []
    for c in col0:
        assert c % HEAD_DIM == 0
        for part in range(3):
            cb = (c + part * w) // HEAD_DIM
            in_specs.append(pl.BlockSpec((seq, HEAD_DIM), lambda n, h, cb=cb: (n, cb + h)))
    n_g = len(groups)
    return pl.pallas_call(
        functools.partial(_dilated_prompt_kernel, groups=groups, seq=seq),
        grid=(n_batch, n_heads),
        in_specs=in_specs,
        out_specs=pl.BlockSpec((seq, HEAD_DIM), lambda n, h: (n, h)),
        out_shape=jax.ShapeDtypeStruct((n_batch * seq, w), BF16),
        scratch_shapes=[pltpu.VMEM((seq, HEAD_DIM), F32) for _ in range(2 * n_g)],
        compiler_params=_params(2),
        name="dilated_prompt",
    )(*([qkv] * (3 * n_g)))


def _export_kernel(k_ref, v_ref, prev_ref, o_ref, *, n_kv):
    del prev_ref
    for h in range(n_kv):
        o_ref[:, 0, h, :] = k_ref[:, _head(h)]
        o_ref[:, 1, h, :] = v_ref[:, _head(h)]


def _export_window(qkv, prev, layer, depth, n_batch, seq, lw, k_col, n_kv):
    wk = n_kv * HEAD_DIM
    rb = min(lw, 256)
    assert lw % rb == 0 and seq % rb == 0 and k_col % wk == 0
    shape = (depth, n_batch, lw, 2, n_kv, HEAD_DIM)
    first = prev is None
    if first:
        prev = jnp.zeros((8, HEAD_DIM), F32)
    row_blk0 = (seq - lw) // rb

    def imap(part):
        return lambda n, c: (n * (seq // rb) + row_blk0 + c, k_col // wk + part)

    return pl.pallas_call(
        functools.partial(_export_kernel, n_kv=n_kv),
        grid=(n_batch, lw // rb),
        in_specs=[pl.BlockSpec((rb, wk), imap(0)), pl.BlockSpec((rb, wk), imap(1)),
                  pl.BlockSpec(memory_space=pl.ANY)],
        out_specs=pl.BlockSpec((None, None, rb, 2, n_kv, HEAD_DIM), lambda n, c: (layer, n, c, 0, 0, 0)),
        out_shape=jax.ShapeDtypeStruct(shape, F32),
        input_output_aliases={} if first else {2: 0},
        compiler_params=_params(2),
        name="export_window",
    )(qkv, qkv, prev)


def _sample_group(q_ref, q_col, new_ref, k_col, c_ref, *, n_kv, grp, dil, lb, max_dist, sink, t_new):
    n_q = n_kv * grp
    wk = n_kv * HEAD_DIM
    rows = n_q * t_new
    n_cache = lb // dil
    n_keys = n_cache + BLOCK
    scale = HEAD_DIM ** -0.5
    assert dil & (dil - 1) == 0 and t_new & (t_new - 1) == 0

    def lane_block(x, idx, n):
        zero = jnp.zeros_like(x)
        return jnp.concatenate([x if i == idx else zero for i in range(n)], axis=1)

    q_bd = jnp.concatenate(
        [lane_block(q_ref[:, q_col + h * HEAD_DIM:q_col + (h + 1) * HEAD_DIM], h // grp, n_kv)
         for h in range(n_q)], axis=0).astype(BF16)
    pad = jnp.zeros((BLOCK - t_new, wk), F32)
    k_new = jnp.concatenate([new_ref[:, k_col:k_col + wk], pad], axis=0)
    v_new = jnp.concatenate([new_ref[:, k_col + wk:k_col + 2 * wk], pad], axis=0)

    t_row = lax.broadcasted_iota(jnp.int32, (rows, n_keys), 0) & (t_new - 1)
    lane = lax.broadcasted_iota(jnp.int32, (rows, n_keys), 1)
    t_col = lax.broadcasted_iota(jnp.int32, (rows, 1), 0) & (t_new - 1)

    o_all = jnp.zeros((rows, wk), F32)
    lse_all = jnp.zeros((rows, 1), F32)
    for r in range(min(dil, t_new)):
        k_c = jnp.concatenate([c_ref[:, 2 * r, h, :] for h in range(n_kv)], axis=1)
        v_c = jnp.concatenate([c_ref[:, 2 * r + 1, h, :] for h in range(n_kv)], axis=1)
        k = jnp.concatenate([k_c, k_new], axis=0).astype(BF16)
        v = jnp.concatenate([v_c, v_new], axis=0).astype(BF16)
        delta = jnp.where(lane < n_cache, lb + t_row - (r + dil * lane), t_row - (lane - n_cache))
        valid = ((delta >= 0) & (delta <= max_dist) & ((delta & (dil - 1)) == 0)
                 & (lane < n_cache + t_new))
        o, lse = _softmax_pv(_dot_nt(q_bd, k) * scale, valid, v, sink)
        mine = ((t_col + lb) & (dil - 1)) == r
        o_all = jnp.where(mine, o, o_all)
        lse_all = jnp.where(mine, lse, lse_all)
    return o_all, lse_all


def _sample_dilated_group(q_ref, new_ref, c_ref, *, n_heads, dil, lb, max_dist, t_new):
    assert n_heads == 8 and dil & (dil - 1) == 0
    h_bits = n_heads.bit_length() - 1
    n_cache = lb // dil
    n_c, n_n = n_cache * n_heads, t_new * n_heads
    scale = HEAD_DIM ** -0.5
    q_all = q_ref[...].reshape(n_n, HEAD_DIM)
    k_new = new_ref[:, 0, :, :].reshape(n_n, HEAD_DIM).astype(BF16)
    v_new = new_ref[:, 1, :, :].reshape(n_n, HEAD_DIM).astype(BF16)
    outs, lses = [None] * t_new, [None] * t_new
    scores = []
    for r in range(min(dil, t_new)):
        ts = [t for t in range(t_new) if (lb + t) % dil == r]
        rows = n_heads * len(ts)
        q = jnp.concatenate([q_all[t * n_heads:(t + 1) * n_heads] for t in ts], axis=0).astype(BF16)
        k_c = c_ref[:, 2 * r, :, :].reshape(n_c, HEAD_DIM).astype(BF16)

        def mask(n_cols, key_row, rows=rows, ts=ts):
            row = lax.broadcasted_iota(jnp.int32, (rows, n_cols), 0)
            col = lax.broadcasted_iota(jnp.int32, (rows, n_cols), 1)
            t_q = ts[0] + dil * (row >> h_bits)
            delta = lb + t_q - key_row(col >> h_bits)
            same_head = (row & (n_heads - 1)) == (col & (n_heads - 1))
            return same_head & (delta >= 0) & (delta <= max_dist) & ((delta & (dil - 1)) == 0)

        s_c = jnp.where(mask(n_c, lambda k, r=r: r + dil * k), _dot_nt(q, k_c) * scale, NEG)
        s_n = jnp.where(mask(n_n, lambda t: lb + t), _dot_nt(q, k_new) * scale, NEG)
        scores.append((r, ts, s_c, s_n))
    probs = []
    for r, ts, s_c, s_n in scores:
        m = jnp.maximum(jnp.max(s_c, axis=-1, keepdims=True), jnp.max(s_n, axis=-1, keepdims=True))
        e_c, e_n = jnp.exp(s_c - m), jnp.exp(s_n - m)
        l = jnp.sum(e_c, axis=-1, keepdims=True) + jnp.sum(e_n, axis=-1, keepdims=True)
        probs.append((r, ts, e_c.astype(BF16), e_n.astype(BF16), l, m + jnp.log(l)))
    for r, ts, e_c, e_n, l, lse in probs:
        v_c = c_ref[:, 2 * r + 1, :, :].reshape(n_c, HEAD_DIM).astype(BF16)
        o = (_dot(e_c, v_c) + _dot(e_n, v_new)) / l
        for i, t in enumerate(ts):
            outs[t] = o[i * n_heads:(i + 1) * n_heads]
            lses[t] = lse[i * n_heads:(i + 1) * n_heads]
    return jnp.concatenate(outs, axis=0), jnp.concatenate(lses, axis=0)


def _sample_kernel(*refs, cfg):
    t_new, n_kv_a, grp_a, n_heads_b, pairs, lbs, lb_a, col_b = (
        cfg["t_new"], cfg["n_kv_a"], cfg["grp_a"], cfg["n_heads_b"], cfg["pairs"], cfg["lbs"],
        cfg["lb_a"], cfg["col_b"])
    n_g = len(pairs)
    qkv_ref, ca_ref = refs[0], refs[1]
    cb_refs = refs[2:2 + n_g]
    sink_ref = refs[2 + n_g]
    pos = 3 + n_g + (1 + n_g)
    oa_ref, ob_ref, na_ref = refs[pos], refs[pos + 1], refs[pos + 2]
    nb_refs = refs[pos + 3:pos + 3 + n_g]
    q_scrs = refs[pos + 3 + n_g:pos + 3 + 2 * n_g]
    mix_scr = refs[pos + 3 + 2 * n_g]

    n_q_a = n_kv_a * grp_a
    w_qa, w_ka, w_b = n_q_a * HEAD_DIM, n_kv_a * HEAD_DIM, n_heads_b * HEAD_DIM

    sink = jnp.concatenate([jnp.full((t_new, 1), sink_ref[h], F32) for h in range(n_q_a)], axis=0)
    o, _ = _sample_group(qkv_ref, 0, qkv_ref, w_qa, ca_ref, n_kv=n_kv_a, grp=grp_a, dil=1, lb=lb_a,
                         max_dist=WINDOW_A - 1, sink=sink, t_new=t_new)
    for h in range(n_q_a):
        oa_ref[:, _head(h)] = o[h * t_new:(h + 1) * t_new, _head(h // grp_a)].astype(oa_ref.dtype)
    for h in range(n_kv_a):
        na_ref[:, 0, h, :] = qkv_ref[:, w_qa + h * HEAD_DIM:w_qa + (h + 1) * HEAD_DIM]
        na_ref[:, 1, h, :] = qkv_ref[:, w_qa + w_ka + h * HEAD_DIM:w_qa + w_ka + (h + 1) * HEAD_DIM]

    outs, lses = [], []
    for g, (window, dil) in enumerate(pairs):
        c0 = col_b[g]
        for h in range(n_heads_b):
            nb_refs[g][:, 0, h, :] = qkv_ref[:, c0 + w_b + h * HEAD_DIM:c0 + w_b + (h + 1) * HEAD_DIM]
            nb_refs[g][:, 1, h, :] = qkv_ref[:, c0 + 2 * w_b + h * HEAD_DIM:c0 + 2 * w_b + (h + 1) * HEAD_DIM]
            q_scrs[g][:, h, :] = qkv_ref[:, c0 + h * HEAD_DIM:c0 + (h + 1) * HEAD_DIM]
        o, lse = _sample_dilated_group(q_scrs[g], nb_refs[g], cb_refs[g], n_heads=n_heads_b, dil=dil,
                                       lb=lbs[g], max_dist=window, t_new=t_new)
        outs.append(o)
        lses.append(lse)
    m = functools.reduce(jnp.maximum, lses)
    ws = [jnp.exp(l - m) for l in lses]
    tot = functools.reduce(jnp.add, ws)
    mix = functools.reduce(jnp.add, [(w / tot) * o for w, o in zip(ws, outs)])
    mix_scr[...] = mix.reshape(t_new, n_heads_b, HEAD_DIM)
    for h in range(n_heads_b):
        ob_ref[:, _head(h)] = mix_scr[:, h, :].astype(ob_ref.dtype)


def _sample_step(qkv, row0, cache_a, caches_b, sink, prev_new, layer, pairs, col_b, t_new):
    depth, n_req, lb_a, _, n_kv_a, _ = cache_a.shape
    n_q_a = sink.shape[0]
    n_heads_b = caches_b[0].shape[4]
    in_width = qkv.shape[1]
    assert row0 % t_new == 0 and t_new % 8 == 0
    lbs = tuple(c.shape[2] for c in caches_b)
    cfg = dict(t_new=t_new, n_kv_a=n_kv_a, grp_a=n_q_a // n_kv_a, n_heads_b=n_heads_b, pairs=pairs,
               lbs=lbs, lb_a=lb_a, col_b=col_b)

    def cache_spec(cache, lb, dil):
        n_kv = cache.shape[4]
        assert lb % dil == 0 and lb // dil == BLOCK
        view = cache.reshape(depth, n_req, lb // dil, dil * 2, n_kv, HEAD_DIM)
        n_cls = min(dil, t_new)
        return view, pl.BlockSpec((None, None, lb // dil, n_cls * 2, n_kv, HEAD_DIM),
                                  lambda b: (layer, b, 0, 0, 0, 0))

    args = [qkv]
    in_specs = [pl.BlockSpec((t_new, in_width), lambda b: (row0 // t_new + b, 0))]
    view, spec = cache_spec(cache_a, lb_a, 1)
    args.append(view)
    in_specs.append(spec)
    for cache, lb, (window, dil) in zip(caches_b, lbs, pairs):
        view, spec = cache_spec(cache, lb, dil)
        args.append(view)
        in_specs.append(spec)
    args.append(sink)
    in_specs.append(pl.BlockSpec(memory_space=pltpu.SMEM))

    new_shapes = [(depth, n_req, t_new, 2, n_kv_a, HEAD_DIM)] + [(depth, n_req, t_new, 2, n_heads_b, HEAD_DIM)] * len(pairs)
    first = prev_new is None
    if first:
        prev_new = [jnp.zeros((8, HEAD_DIM), F32) for _ in new_shapes]
    n_in = len(args)
    args += list(prev_new)
    in_specs += [pl.BlockSpec(memory_space=pl.ANY)] * len(prev_new)

    n_s = n_req * t_new
    out_shape = [jax.ShapeDtypeStruct((n_s, n_q_a * HEAD_DIM), BF16),
                 jax.ShapeDtypeStruct((n_s, n_heads_b * HEAD_DIM), BF16)]
    out_specs = [pl.BlockSpec((t_new, n_q_a * HEAD_DIM), lambda b: (b, 0)),
                 pl.BlockSpec((t_new, n_heads_b * HEAD_DIM), lambda b: (b, 0))]
    for s in new_shapes:
        out_shape.append(jax.ShapeDtypeStruct(s, F32))
        out_specs.append(pl.BlockSpec((None, None) + s[2:], lambda b: (layer, b, 0, 0, 0, 0)))
    aliases = {} if first else {n_in + i: 2 + i for i in range(len(new_shapes))}
    outs = pl.pallas_call(
        functools.partial(_sample_kernel, cfg=cfg),
        grid=(n_req,),
        in_specs=in_specs,
        out_specs=out_specs,
        out_shape=out_shape,
        input_output_aliases=aliases,
        scratch_shapes=[pltpu.VMEM((t_new, n_heads_b, HEAD_DIM), F32) for _ in range(len(pairs) + 1)],
        compiler_params=_params(1),
        name="sample_step",
    )(*args)
    return outs[0], outs[1], list(outs[2:])


def _rope_tables(positions):
    half = HEAD_DIM // 2
    inv = ROPE_THETA ** (-jnp.arange(half, dtype=F32) / half)
    ang = positions.astype(F32)[:, None] * inv[None, :]
    cos, sin = jnp.cos(ang), jnp.sin(ang)
    return jnp.concatenate([cos, cos], axis=1), jnp.concatenate([-sin, sin], axis=1)


def kernel(x_prompt, x_sample, cache_a_kv, cache_b1_kv, cache_b2_kv, cache_b3_kv, norm_ffn1, w_ffn1_in,
           w_ffn1_out, norm_mix, w_in, sinks, w_gate, w_up_a, w_up_b, w_o, norm_ffn2, w_ffn2_in,
           w_ffn2_out, norm_final):
    batch, seq, d_model = x_prompt.shape
    n_req, t_new, _ = x_sample.shape
    depth = w_in.shape[0]
    n_heads_a = sinks.shape[1]
    n_kv_a = cache_a_kv.shape[4]
    grp_a = n_heads_a // n_kv_a
    n_heads_b = cache_b1_kv.shape[4]
    w_qa, w_ka, w_b = n_heads_a * HEAD_DIM, n_kv_a * HEAD_DIM, n_heads_b * HEAD_DIM
    in_width = w_in.shape[2]
    assert in_width == w_qa + 2 * w_ka + 9 * w_b
    caches_b = (cache_b1_kv, cache_b2_kv, cache_b3_kv)

    n_p, n_s = batch * seq, n_req * t_new
    n_tok = n_p + n_s
    tm = n_tok // 8
    assert n_tok % (8 * 32) == 0

    x = jnp.concatenate([x_prompt.reshape(n_p, d_model), x_sample.reshape(n_s, d_model)], axis=0)
    positions = jnp.concatenate([jnp.tile(jnp.arange(seq), batch),
                                 jnp.tile(PAST_LEN + jnp.arange(t_new), n_req)])
    cos_t, sin_t = _rope_tables(positions)

    tn_qkv = 1024
    seg = [(w_qa, True), (w_ka, True), (w_ka, False)] + [(w_b, True), (w_b, True), (w_b, False)] * 3
    rope_flags = tuple(int(f) for w, f in seg for _ in range(w // HEAD_DIM))
    assert len(rope_flags) == in_width // HEAD_DIM
    col_b = tuple(w_qa + 2 * w_ka + 3 * w_b * g for g in range(3))

    new_p = [None] * (1 + len(DILATED_PAIRS))
    new_s = None
    tm_norm = n_s
    for l in range(depth):
        h = _rmsnorm(x, norm_ffn1[l], BF16, tm_norm)
        x = _ffn(x, h, w_ffn1_in, w_ffn1_out, l, tm)

        h = _rmsnorm(x, norm_mix[l], BF16, tm_norm)
        qkv = _matmul([h], [(w_in, l, 0, 0)],
                      [(cos_t, (tm, HEAD_DIM), lambda j, i: (i, 0)), (sin_t, (tm, HEAD_DIM), lambda j, i: (i, 0))],
                      functools.partial(_rope_epilogue, tn=tn_qkv, rope_flags=rope_flags),
                      in_width, F32, tm, tn_qkv, "qkv_rope", n_split=2)

        oa_p = _swa_prompt(qkv, batch, seq, n_kv_a, grp_a, WINDOW_A - 1, sinks[l])
        ob_p = _dilated_prompt(qkv, batch, seq, col_b, n_heads_b, DILATED_PAIRS)
        oa_s, ob_s, new_s = _sample_step(qkv, n_p, cache_a_kv, caches_b, sinks[l], new_s, l,
                                         DILATED_PAIRS, col_b, t_new)
        new_p[0] = _export_window(qkv, new_p[0], l, depth, batch, seq, min(WINDOW_A, seq), w_qa, n_kv_a)
        for g, (window, dil) in enumerate(DILATED_PAIRS):
            new_p[1 + g] = _export_window(qkv, new_p[1 + g], l, depth, batch, seq, min(window, seq),
                                          col_b[g] + w_b, n_heads_b)
        o_a = jnp.concatenate([oa_p, oa_s], axis=0)
        o_b = jnp.concatenate([ob_p, ob_s], axis=0)

        tn_g = 256
        u = _matmul([h, o_a, o_b],
                    [(w_gate, l, 0, 0), (w_gate, l, 0, d_model // tn_g), (w_up_a, l, 1, 0), (w_up_b, l, 2, 0)],
                    [], _gate_epilogue, d_model, BF16, tm, tn_g, "gate_up")
        x = _matmul([u], [(w_o, l, 0, 0)], [(x, (tm, 512), lambda j, i: (i, j))],
                    functools.partial(_residual_epilogue, scale=1.0), d_model, F32, tm, 512, "w_o", n_split=2)

        h = _rmsnorm(x, norm_ffn2[l], BF16, tm_norm)
        x = _ffn(x, h, w_ffn2_in, w_ffn2_out, l, tm)

    y_prompt = _rmsnorm(x, norm_final, F32, tm_norm, 0, n_p).reshape(batch, seq, d_model)
    y_sample = _rmsnorm(x, norm_final, F32, tm_norm, n_p, n_s).reshape(n_req, t_new, d_model)
    return (y_prompt, y_sample, new_p[0], new_s[0], new_p[1], new_s[1], new_p[2], new_s[2], new_p[3], new_s[3])
```

```python
import functools

import jax
import jax.numpy as jnp
from jax import lax
from jax.experimental import pallas as pl
from jax.experimental.pallas import tpu as pltpu

HEAD_DIM = 128
BLOCK = 128
WINDOW_A = 128
PAST_LEN = 16384
DILATED_PAIRS = ((128, 1), (512, 4), (2048, 16))
ROPE_THETA = 10000.0
EPS = 1e-5
NEG = -1e30
VMEM_LIMIT = 56 * 1024 * 1024
UNITS_IN_FLIGHT = 8

F32 = jnp.float32
BF16 = jnp.bfloat16


def _params(n_axes, vmem=VMEM_LIMIT):
    return pltpu.CompilerParams(dimension_semantics=("arbitrary",) * n_axes, vmem_limit_bytes=vmem)


def _dot(a, b):
    return jnp.dot(a, b, preferred_element_type=F32)


def _dot_nt(a, b):
    return lax.dot_general(a, b, (((1,), (1,)), ((), ())), preferred_element_type=F32)


def _head(h):
    return slice(h * HEAD_DIM, (h + 1) * HEAD_DIM)


def _rmsnorm_kernel(x_ref, g_ref, o_ref):
    x = x_ref[...]
    y = x * lax.rsqrt(jnp.mean(x * x, axis=-1, keepdims=True) + EPS)
    o_ref[...] = (y * g_ref[...]).astype(o_ref.dtype)


def _rmsnorm(x, g, out_dtype, tm, row0=0, rows=None):
    d = x.shape[1]
    rows = x.shape[0] if rows is None else rows
    assert rows % tm == 0 and row0 % tm == 0
    blk0 = row0 // tm
    return pl.pallas_call(
        _rmsnorm_kernel,
        grid=(rows // tm,),
        in_specs=[pl.BlockSpec((tm, d), lambda i: (blk0 + i, 0)), pl.BlockSpec((1, d), lambda i: (0, 0))],
        out_specs=pl.BlockSpec((tm, d), lambda i: (i, 0)),
        out_shape=jax.ShapeDtypeStruct((rows, d), out_dtype),
        compiler_params=_params(1),
        name="rmsnorm",
    )(x, g.reshape(1, d))


def _mm_kernel(*refs, n_a, b_to_a, b_is_f32, n_extra, has_side, epilogue, n_split, lead):
    n_b = len(b_to_a)
    a_refs = refs[:n_a]
    b_refs = refs[n_a:n_a + n_b]
    pos = n_a + n_b
    extra_refs = refs[pos:pos + n_extra]
    pos += n_extra
    side_in = refs[pos] if has_side else None
    pos += int(has_side)
    o_ref = refs[pos]
    side_out = refs[pos + 1] if has_side else None
    scratch = list(refs[pos + 1 + int(has_side):])
    w_refs = [scratch.pop(0) if f32 else b_ref for b_ref, f32 in zip(b_refs, b_is_f32)]
    jj, i = pl.program_id(0), pl.program_id(1)

    if has_side:
        side_out[...] = side_in[...].astype(BF16)

    def fill():
        for b_ref, w_ref, f32 in zip(b_refs, w_refs, b_is_f32):
            if f32:
                kc = b_ref.shape[0]
                w_ref[jj % 2, pl.ds(pl.multiple_of(i * kc, kc), kc), :] = b_ref[...].astype(BF16)

    def compute():
        ws = [w_ref[(jj + 1) % 2] if f32 else w_ref[...] for w_ref, f32 in zip(w_refs, b_is_f32)]
        rows_per = o_ref.shape[0] // n_split
        for s in range(n_split):
            rows = pl.ds(s * rows_per, rows_per)
            accs = [_dot(a_refs[ai][rows, :], ws[k]) for k, ai in enumerate(b_to_a)]
            out = epilogue(accs, [e[rows, :] for e in extra_refs], jj - lead)
            o_ref[rows, :] = out.astype(o_ref.dtype)

    if lead:
        pl.when(jj == 0)(fill)

        @pl.when(jj > 0)
        def _():
            fill()
            compute()
    else:
        compute()


def _matmul(a_list, b_list, extras, epilogue, n_out, out_dtype, tm, tn, name, n_split=1, side=None):
    m = a_list[0].shape[0]
    n_j, n_i = n_out // tn, m // tm
    assert m % tm == 0 and n_out % tn == 0 and tm % (16 * n_split) == 0
    lead = int(any(b[0].ndim == 3 for b in b_list))

    def col(jj):
        return jnp.maximum(jj - lead, 0)

    def row(jj, i):
        return jnp.where(jj < lead, 0, i)

    in_specs = [pl.BlockSpec((tm, a.shape[1]), lambda jj, i: (row(jj, i), 0)) for a in a_list]
    args = list(a_list)
    scratch = []
    for w, layer, ai, off in b_list:
        k = a_list[ai].shape[1]
        if w.ndim == 3:
            assert w.shape[1] == k and w.shape[2] % tn == 0 and k % (16 * n_i) == 0
            in_specs.append(pl.BlockSpec(
                (None, k // n_i, tn),
                lambda jj, i, layer=layer, off=off: (layer, i, jnp.minimum(jj, n_j - 1) + off)))
            args.append(w)
            scratch.append(pltpu.VMEM((2, k, tn), BF16))
        else:
            assert w.dtype == BF16 and w.shape[0] == k and w.shape[1] % tn == 0
            in_specs.append(pl.BlockSpec((k, tn), lambda jj, i, off=off: (0, col(jj) + off)))
            args.append(w)
    for arr, shape, imap in extras:
        assert shape[0] == tm
        in_specs.append(pl.BlockSpec(shape, lambda jj, i, imap=imap: imap(col(jj), row(jj, i))))
        args.append(arr)
    out_shape = [jax.ShapeDtypeStruct((m, n_out), out_dtype)]
    out_specs = [pl.BlockSpec((tm, tn), lambda jj, i: (row(jj, i), col(jj)))]
    if side is not None:
        w_side, layer = side
        _, r, c = w_side.shape
        rs = r // (n_j * n_i)
        assert r % (n_j * n_i) == 0 and rs % 16 == 0
        in_specs.append(pl.BlockSpec((None, rs, c), lambda jj, i: (layer, col(jj) * n_i + row(jj, i), 0)))
        out_shape.append(jax.ShapeDtypeStruct((r, c), BF16))
        out_specs.append(pl.BlockSpec((rs, c), lambda jj, i: (col(jj) * n_i + row(jj, i), 0)))
        args.append(w_side)
    kernel = functools.partial(_mm_kernel, n_a=len(a_list), b_to_a=tuple(b[2] for b in b_list),
                               b_is_f32=tuple(b[0].ndim == 3 for b in b_list),
                               n_extra=len(extras), has_side=side is not None, epilogue=epilogue,
                               n_split=n_split, lead=lead)
    outs = pl.pallas_call(
        kernel,
        grid=(n_j + lead, m // tm),
        in_specs=in_specs,
        out_specs=out_specs,
        out_shape=out_shape,
        scratch_shapes=scratch,
        compiler_params=_params(2),
        name=name,
    )(*args)
    return outs[0] if side is None else outs


def _swiglu_epilogue(accs, extras, j):
    a, b = accs
    return a * jax.nn.sigmoid(a) * b


def _residual_epilogue(accs, extras, j, *, scale):
    return extras[0] + scale * accs[0]


def _gate_epilogue(accs, extras, j):
    g_a, g_b, u_a, u_b = accs
    return jax.nn.sigmoid(g_a) * u_a + jax.nn.sigmoid(g_b) * u_b


def _rope_epilogue(accs, extras, j, *, tn, rope_flags):
    acc = accs[0]
    cos, sin = extras
    per_tile = tn // HEAD_DIM
    n_tiles = len(rope_flags) // per_tile
    assert n_tiles < 31 and len(rope_flags) == n_tiles * per_tile
    heads = []
    for h in range(per_tile):
        flag_bits = sum(int(rope_flags[t * per_tile + h]) << t for t in range(n_tiles))
        bit = lax.shift_right_logical(jnp.int32(flag_bits), j) & 1
        on = jnp.broadcast_to(bit, (1, HEAD_DIM)).astype(F32)
        x = acc[:, _head(h)]
        heads.append(x * (on * cos + (1.0 - on)) + pltpu.roll(x, HEAD_DIM // 2, 1) * (on * sin))
    return jnp.concatenate(heads, axis=1)


def _ffn(x, h, w_in, w_out, layer, tm):
    tf = 256
    d_ff = w_out.shape[1]
    assert d_ff % tf == 0
    g, w_out_bf16 = _matmul([h], [(w_in, layer, 0, 0), (w_in, layer, 0, d_ff // tf)], [], _swiglu_epilogue,
                            d_ff, BF16, 2 * tm, tf, "ffn_in", n_split=2, side=(w_out, layer))
    tn = 512
    return _matmul([g], [(w_out_bf16, 0, 0, 0)], [(x, (tm // 2, tn), lambda j, i: (i, j))],
                   functools.partial(_residual_epilogue, scale=0.5), x.shape[1], F32, tm // 2, tn, "ffn_out")


def _softmax_pv(s, valid, v, sink=None):
    s = jnp.where(valid, s, NEG)
    m = jnp.max(s, axis=-1, keepdims=True)
    if sink is not None:
        m = jnp.maximum(m, sink)
    e = jnp.exp(s - m)
    l = jnp.sum(e, axis=-1, keepdims=True)
    denom = l if sink is None else l + jnp.exp(sink - m)
    return _dot(e.astype(BF16), v) / denom, m + jnp.log(l)


def _swa_prompt_kernel(q_ref, kc_ref, kp_ref, vc_ref, vp_ref, sink_ref, o_ref, *, n_kv, grp, max_dist):
    assert max_dist < BLOCK
    rows = grp * BLOCK
    scale = HEAD_DIM ** -0.5
    prev_on = jnp.where(pl.program_id(1) == 0, 0, 1)
    qi = lax.broadcasted_iota(jnp.int32, (rows, BLOCK), 0) & (BLOCK - 1)
    kj = lax.broadcasted_iota(jnp.int32, (rows, BLOCK), 1)
    tri = kj <= qi
    valid = (jnp.where(tri, qi - kj, qi + BLOCK - kj) <= max_dist) & (kj <= qi + prev_on * BLOCK)

    for kh in range(n_kv):
        q = jnp.concatenate([q_ref[:, _head(kh * grp + g)] for g in range(grp)], axis=0).astype(BF16)
        s_cur = _dot_nt(q, kc_ref[:, _head(kh)].astype(BF16))
        s_prev = _dot_nt(q, kp_ref[:, _head(kh)].astype(BF16))
        s = jnp.where(valid, jnp.where(tri, s_cur, s_prev) * scale, NEG)
        sink = jnp.concatenate(
            [jnp.full((BLOCK, 1), sink_ref[kh * grp + g], F32) for g in range(grp)], axis=0)
        m = jnp.maximum(jnp.max(s, axis=-1, keepdims=True), sink)
        e = jnp.exp(s - m)
        denom = jnp.sum(e, axis=-1, keepdims=True) + jnp.exp(sink - m)
        o = (_dot(jnp.where(tri, e, 0.0).astype(BF16), vc_ref[:, _head(kh)].astype(BF16))
             + _dot(jnp.where(tri, 0.0, e).astype(BF16), vp_ref[:, _head(kh)].astype(BF16))) / denom
        for g in range(grp):
            o_ref[:, _head(kh * grp + g)] = o[g * BLOCK:(g + 1) * BLOCK].astype(o_ref.dtype)


def _swa_prompt(qkv, n_batch, seq, n_kv, grp, max_dist, sink):
    wq, wk = n_kv * grp * HEAD_DIM, n_kv * HEAD_DIM
    assert seq % BLOCK == 0 and wq % wk == 0
    nbl = seq // BLOCK
    k_blk, v_blk = wq // wk, wq // wk + 1

    def imap(col, prev):
        return lambda n, b: (n * nbl + (jnp.maximum(b - 1, 0) if prev else b), col)

    return pl.pallas_call(
        functools.partial(_swa_prompt_kernel, n_kv=n_kv, grp=grp, max_dist=max_dist),
        grid=(n_batch, nbl),
        in_specs=[pl.BlockSpec((BLOCK, wq), imap(0, False)),
                  pl.BlockSpec((BLOCK, wk), imap(k_blk, False)), pl.BlockSpec((BLOCK, wk), imap(k_blk, True)),
                  pl.BlockSpec((BLOCK, wk), imap(v_blk, False)), pl.BlockSpec((BLOCK, wk), imap(v_blk, True)),
                  pl.BlockSpec(memory_space=pltpu.SMEM)],
        out_specs=pl.BlockSpec((BLOCK, wq), lambda n, b: (n * nbl + b, 0)),
        out_shape=jax.ShapeDtypeStruct((qkv.shape[0], wq), BF16),
        compiler_params=_params(2),
        name="swa_prompt",
    )(qkv, qkv, qkv, qkv, qkv, sink)


def _dilated_prompt_kernel(*refs, groups, seq):
    n_g = len(groups)
    qkv_refs = refs[:3 * n_g]
    o_ref = refs[3 * n_g]
    o_scr = refs[3 * n_g + 1:3 * n_g + 1 + n_g]
    l_scr = refs[3 * n_g + 1 + n_g:]
    scale = HEAD_DIM ** -0.5

    def band(n_keys, off, max_dist):
        qi = lax.broadcasted_iota(jnp.int32, (BLOCK, n_keys), 0)
        kj = lax.broadcasted_iota(jnp.int32, (BLOCK, n_keys), 1)
        dist = qi + off - kj
        return (dist >= 0) & (dist <= max_dist)

    for g, (max_dist, dil) in enumerate(groups):
        q_ref, k_ref, v_ref = qkv_refs[3 * g:3 * g + 3]
        valid_first, valid_next = band(BLOCK, 0, max_dist), band(2 * BLOCK, BLOCK, max_dist)

        def rows(start, dil=dil):
            return pl.ds(start, BLOCK, stride=dil) if dil > 1 else pl.ds(start, BLOCK)

        units = [(r, b) for r in range(dil) for b in range(seq // dil // BLOCK)]
        for u0 in range(0, len(units), UNITS_IN_FLIGHT):
            staged = []
            for r, b in units[u0:u0 + UNITS_IN_FLIGHT]:
                cur = rows(r + b * BLOCK * dil)
                q = q_ref[cur, :].astype(BF16)
                if b == 0:
                    k, v, valid = k_ref[cur, :], v_ref[cur, :], valid_first
                else:
                    prev = rows(r + (b - 1) * BLOCK * dil)
                    k = jnp.concatenate([k_ref[prev, :], k_ref[cur, :]], axis=0)
                    v = jnp.concatenate([v_ref[prev, :], v_ref[cur, :]], axis=0)
                    valid = valid_next
                staged.append((cur, jnp.where(valid, _dot_nt(q, k.astype(BF16)) * scale, NEG), v.astype(BF16)))
            probs = []
            for cur, s, v in staged:
                m = jnp.max(s, axis=-1, keepdims=True)
                e = jnp.exp(s - m)
                l = jnp.sum(e, axis=-1, keepdims=True)
                probs.append((cur, e.astype(BF16), v, l, m))
            for cur, e, v, l, m in probs:
                o_scr[g][cur, :] = _dot(e, v) / l
                l_scr[g][cur, :] = jnp.broadcast_to(m + jnp.log(l), (BLOCK, HEAD_DIM))

    chunk = 2 * BLOCK
    for c in range(seq // chunk):
        cs = pl.ds(c * chunk, chunk)
        lses = [l[cs, :] for l in l_scr]
        m = functools.reduce(jnp.maximum, lses)
        ws = [jnp.exp(l - m) for l in lses]
        tot = functools.reduce(jnp.add, ws)
        mix = functools.reduce(jnp.add, [(w / tot) * o[cs, :] for w, o in zip(ws, o_scr)])
        o_ref[cs, :] = mix.astype(o_ref.dtype)


def _dilated_prompt(qkv, n_batch, seq, col0, n_heads, pairs):
    groups = tuple((window // dil, dil) for window, dil in pairs)
    assert all(seq % (dil * BLOCK) == 0 and window // dil <= BLOCK for window, dil in pairs)
    w = n_heads * HEAD_DIM
    in_specs = []
    for c in col0:
        assert c % HEAD_DIM == 0
        for part in range(3):
            cb = (c + part * w) // HEAD_DIM
            in_specs.append(pl.BlockSpec((seq, HEAD_DIM), lambda n, h, cb=cb: (n, cb + h)))
    n_g = len(groups)
    return pl.pallas_call(
        functools.partial(_dilated_prompt_kernel, groups=groups, seq=seq),
        grid=(n_batch, n_heads),
        in_specs=in_specs,
        out_specs=pl.BlockSpec((seq, HEAD_DIM), lambda n, h: (n, h)),
        out_shape=jax.ShapeDtypeStruct((qkv.shape[0], w), BF16),
        scratch_shapes=[pltpu.VMEM((seq, HEAD_DIM), F32) for _ in range(2 * n_g)],
        compiler_params=_params(2),
        name="dilated_prompt",
    )(*([qkv] * (3 * n_g)))


def _export_kernel(k_ref, v_ref, prev_ref, o_ref, *, n_kv):
    del prev_ref
    for h in range(n_kv):
        o_ref[:, 0, h, :] = k_ref[:, _head(h)]
        o_ref[:, 1, h, :] = v_ref[:, _head(h)]


def _export_window(qkv, prev, layer, depth, n_batch, seq, lw, k_col, n_kv):
    wk = n_kv * HEAD_DIM
    rb = min(lw, 256)
    assert lw % rb == 0 and seq % rb == 0 and k_col % wk == 0
    shape = (depth, n_batch, lw, 2, n_kv, HEAD_DIM)
    first = prev is None
    if first:
        prev = jnp.zeros((8, HEAD_DIM), F32)
    row_blk0 = (seq - lw) // rb

    def imap(part):
        return lambda n, c: (n * (seq // rb) + row_blk0 + c, k_col // wk + part)

    return pl.pallas_call(
        functools.partial(_export_kernel, n_kv=n_kv),
        grid=(n_batch, lw // rb),
        in_specs=[pl.BlockSpec((rb, wk), imap(0)), pl.BlockSpec((rb, wk), imap(1)),
                  pl.BlockSpec(memory_space=pl.ANY)],
        out_specs=pl.BlockSpec((None, None, rb, 2, n_kv, HEAD_DIM), lambda n, c: (layer, n, c, 0, 0, 0)),
        out_shape=jax.ShapeDtypeStruct(shape, F32),
        input_output_aliases={} if first else {2: 0},
        compiler_params=_params(2),
        name="export_window",
    )(qkv, qkv, prev)


def _sample_group(q_ref, q_col, new_ref, k_col, c_ref, *, n_kv, grp, dil, lb, max_dist, sink, t_new):
    n_q = n_kv * grp
    wk = n_kv * HEAD_DIM
    rows = n_q * t_new
    n_cache = lb // dil
    n_keys = n_cache + BLOCK
    scale = HEAD_DIM ** -0.5
    assert dil & (dil - 1) == 0 and t_new & (t_new - 1) == 0

    def lane_block(x, idx, n):
        zero = jnp.zeros_like(x)
        return jnp.concatenate([x if i == idx else zero for i in range(n)], axis=1)

    q_bd = jnp.concatenate(
        [lane_block(q_ref[:, q_col + h * HEAD_DIM:q_col + (h + 1) * HEAD_DIM], h // grp, n_kv)
         for h in range(n_q)], axis=0).astype(BF16)
    pad = jnp.zeros((BLOCK - t_new, wk), F32)
    k_new = jnp.concatenate([new_ref[:, k_col:k_col + wk], pad], axis=0)
    v_new = jnp.concatenate([new_ref[:, k_col + wk:k_col + 2 * wk], pad], axis=0)

    t_row = lax.broadcasted_iota(jnp.int32, (rows, n_keys), 0) & (t_new - 1)
    lane = lax.broadcasted_iota(jnp.int32, (rows, n_keys), 1)
    t_col = lax.broadcasted_iota(jnp.int32, (rows, 1), 0) & (t_new - 1)

    o_all = jnp.zeros((rows, wk), F32)
    lse_all = jnp.zeros((rows, 1), F32)
    for r in range(min(dil, t_new)):
        k_c = jnp.concatenate([c_ref[:, 2 * r, h, :] for h in range(n_kv)], axis=1)
        v_c = jnp.concatenate([c_ref[:, 2 * r + 1, h, :] for h in range(n_kv)], axis=1)
        k = jnp.concatenate([k_c, k_new], axis=0).astype(BF16)
        v = jnp.concatenate([v_c, v_new], axis=0).astype(BF16)
        delta = jnp.where(lane < n_cache, lb + t_row - (r + dil * lane), t_row - (lane - n_cache))
        valid = ((delta >= 0) & (delta <= max_dist) & ((delta & (dil - 1)) == 0)
                 & (lane < n_cache + t_new))
        o, lse = _softmax_pv(_dot_nt(q_bd, k) * scale, valid, v, sink)
        mine = ((t_col + lb) & (dil - 1)) == r
        o_all = jnp.where(mine, o, o_all)
        lse_all = jnp.where(mine, lse, lse_all)
    return o_all, lse_all


def _sample_dilated_group(q_ref, new_ref, c_ref, *, n_heads, dil, lb, max_dist, t_new):
    assert n_heads == 8 and dil & (dil - 1) == 0
    h_bits = n_heads.bit_length() - 1
    n_cache = lb // dil
    n_c, n_n = n_cache * n_heads, t_new * n_heads
    scale = HEAD_DIM ** -0.5
    q_all = q_ref[...].reshape(n_n, HEAD_DIM)
    k_new = new_ref[:, 0, :, :].reshape(n_n, HEAD_DIM).astype(BF16)
    v_new = new_ref[:, 1, :, :].reshape(n_n, HEAD_DIM).astype(BF16)
    outs, lses = [None] * t_new, [None] * t_new
    scores = []
    for r in range(min(dil, t_new)):
        ts = [t for t in range(t_new) if (lb + t) % dil == r]
        rows = n_heads * len(ts)
        q = jnp.concatenate([q_all[t * n_heads:(t + 1) * n_heads] for t in ts], axis=0).astype(BF16)
        k_c = c_ref[:, 2 * r, :, :].reshape(n_c, HEAD_DIM).astype(BF16)

        def mask(n_cols, key_row, rows=rows, ts=ts):
            row = lax.broadcasted_iota(jnp.int32, (rows, n_cols), 0)
            col = lax.broadcasted_iota(jnp.int32, (rows, n_cols), 1)
            t_q = ts[0] + dil * (row >> h_bits)
            delta = lb + t_q - key_row(col >> h_bits)
            same_head = (row & (n_heads - 1)) == (col & (n_heads - 1))
            return same_head & (delta >= 0) & (delta <= max_dist) & ((delta & (dil - 1)) == 0)

        s_c = jnp.where(mask(n_c, lambda k, r=r: r + dil * k), _dot_nt(q, k_c) * scale, NEG)
        s_n = jnp.where(mask(n_n, lambda t: lb + t), _dot_nt(q, k_new) * scale, NEG)
        scores.append((r, ts, s_c, s_n))
    probs = []
    for r, ts, s_c, s_n in scores:
        m = jnp.maximum(jnp.max(s_c, axis=-1, keepdims=True), jnp.max(s_n, axis=-1, keepdims=True))
        e_c, e_n = jnp.exp(s_c - m), jnp.exp(s_n - m)
        l = jnp.sum(e_c, axis=-1, keepdims=True) + jnp.sum(e_n, axis=-1, keepdims=True)
        probs.append((r, ts, e_c.astype(BF16), e_n.astype(BF16), l, m + jnp.log(l)))
    for r, ts, e_c, e_n, l, lse in probs:
        v_c = c_ref[:, 2 * r + 1, :, :].reshape(n_c, HEAD_DIM).astype(BF16)
        o = (_dot(e_c, v_c) + _dot(e_n, v_new)) / l
        for i, t in enumerate(ts):
            outs[t] = o[i * n_heads:(i + 1) * n_heads]
            lses[t] = lse[i * n_heads:(i + 1) * n_heads]
    return jnp.concatenate(outs, axis=0), jnp.concatenate(lses, axis=0)


def _sample_kernel(*refs, cfg):
    t_new, n_kv_a, grp_a, n_heads_b, pairs, lbs, lb_a, col_b = (
        cfg["t_new"], cfg["n_kv_a"], cfg["grp_a"], cfg["n_heads_b"], cfg["pairs"], cfg["lbs"],
        cfg["lb_a"], cfg["col_b"])
    n_g = len(pairs)
    qkv_ref, ca_ref = refs[0], refs[1]
    cb_refs = refs[2:2 + n_g]
    sink_ref = refs[2 + n_g]
    pos = 3 + n_g + (1 + n_g) + 2
    oa_ref, ob_ref, na_ref = refs[pos], refs[pos + 1], refs[pos + 2]
    nb_refs = refs[pos + 3:pos + 3 + n_g]
    q_scrs = refs[pos + 3 + n_g:pos + 3 + 2 * n_g]
    mix_scr = refs[pos + 3 + 2 * n_g]

    n_q_a = n_kv_a * grp_a
    w_qa, w_ka, w_b = n_q_a * HEAD_DIM, n_kv_a * HEAD_DIM, n_heads_b * HEAD_DIM

    sink = jnp.concatenate([jnp.full((t_new, 1), sink_ref[h], F32) for h in range(n_q_a)], axis=0)
    o, _ = _sample_group(qkv_ref, 0, qkv_ref, w_qa, ca_ref, n_kv=n_kv_a, grp=grp_a, dil=1, lb=lb_a,
                         max_dist=WINDOW_A - 1, sink=sink, t_new=t_new)
    for h in range(n_q_a):
        oa_ref[:, _head(h)] = o[h * t_new:(h + 1) * t_new, _head(h // grp_a)].astype(oa_ref.dtype)
    for h in range(n_kv_a):
        na_ref[:, 0, h, :] = qkv_ref[:, w_qa + h * HEAD_DIM:w_qa + (h + 1) * HEAD_DIM]
        na_ref[:, 1, h, :] = qkv_ref[:, w_qa + w_ka + h * HEAD_DIM:w_qa + w_ka + (h + 1) * HEAD_DIM]

    outs, lses = [], []
    for g, (window, dil) in enumerate(pairs):
        c0 = col_b[g]
        for h in range(n_heads_b):
            nb_refs[g][:, 0, h, :] = qkv_ref[:, c0 + w_b + h * HEAD_DIM:c0 + w_b + (h + 1) * HEAD_DIM]
            nb_refs[g][:, 1, h, :] = qkv_ref[:, c0 + 2 * w_b + h * HEAD_DIM:c0 + 2 * w_b + (h + 1) * HEAD_DIM]
            q_scrs[g][:, h, :] = qkv_ref[:, c0 + h * HEAD_DIM:c0 + (h + 1) * HEAD_DIM]
        o, lse = _sample_dilated_group(q_scrs[g], nb_refs[g], cb_refs[g], n_heads=n_heads_b, dil=dil,
                                       lb=lbs[g], max_dist=window, t_new=t_new)
        outs.append(o)
        lses.append(lse)
    m = functools.reduce(jnp.maximum, lses)
    ws = [jnp.exp(l - m) for l in lses]
    tot = functools.reduce(jnp.add, ws)
    mix = functools.reduce(jnp.add, [(w / tot) * o for w, o in zip(ws, outs)])
    mix_scr[...] = mix.reshape(t_new, n_heads_b, HEAD_DIM)
    for h in range(n_heads_b):
        ob_ref[:, _head(h)] = mix_scr[:, h, :].astype(ob_ref.dtype)


def _sample_step(qkv, row0, cache_a, caches_b, sink, prev_new, layer, pairs, col_b, t_new, o_a, o_b):
    depth, n_req, lb_a, _, n_kv_a, _ = cache_a.shape
    n_q_a = sink.shape[0]
    n_heads_b = caches_b[0].shape[4]
    in_width = qkv.shape[1]
    assert row0 % t_new == 0 and t_new % 8 == 0
    lbs = tuple(c.shape[2] for c in caches_b)
    cfg = dict(t_new=t_new, n_kv_a=n_kv_a, grp_a=n_q_a // n_kv_a, n_heads_b=n_heads_b, pairs=pairs,
               lbs=lbs, lb_a=lb_a, col_b=col_b)

    def cache_spec(cache, lb, dil):
        n_kv = cache.shape[4]
        assert lb % dil == 0 and lb // dil == BLOCK
        view = cache.reshape(depth, n_req, lb // dil, dil * 2, n_kv, HEAD_DIM)
        n_cls = min(dil, t_new)
        return view, pl.BlockSpec((None, None, lb // dil, n_cls * 2, n_kv, HEAD_DIM),
                                  lambda b: (layer, b, 0, 0, 0, 0))

    args = [qkv]
    in_specs = [pl.BlockSpec((t_new, in_width), lambda b: (row0 // t_new + b, 0))]
    view, spec = cache_spec(cache_a, lb_a, 1)
    args.append(view)
    in_specs.append(spec)
    for cache, lb, (window, dil) in zip(caches_b, lbs, pairs):
        view, spec = cache_spec(cache, lb, dil)
        args.append(view)
        in_specs.append(spec)
    args.append(sink)
    in_specs.append(pl.BlockSpec(memory_space=pltpu.SMEM))

    new_shapes = [(depth, n_req, t_new, 2, n_kv_a, HEAD_DIM)] + [(depth, n_req, t_new, 2, n_heads_b, HEAD_DIM)] * len(pairs)
    first = prev_new is None
    if first:
        prev_new = [jnp.zeros((8, HEAD_DIM), F32) for _ in new_shapes]
    n_in = len(args)
    args += list(prev_new)
    in_specs += [pl.BlockSpec(memory_space=pl.ANY)] * len(prev_new)

    assert o_a.shape == (qkv.shape[0], n_q_a * HEAD_DIM) and o_b.shape == (qkv.shape[0], n_heads_b * HEAD_DIM)
    args += [o_a, o_b]
    in_specs += [pl.BlockSpec(memory_space=pl.ANY)] * 2
    out_shape = [jax.ShapeDtypeStruct(o_a.shape, o_a.dtype), jax.ShapeDtypeStruct(o_b.shape, o_b.dtype)]
    out_specs = [pl.BlockSpec((t_new, o_a.shape[1]), lambda b: (row0 // t_new + b, 0)),
                 pl.BlockSpec((t_new, o_b.shape[1]), lambda b: (row0 // t_new + b, 0))]
    for s in new_shapes:
        out_shape.append(jax.ShapeDtypeStruct(s, F32))
        out_specs.append(pl.BlockSpec((None, None) + s[2:], lambda b: (layer, b, 0, 0, 0, 0)))
    aliases = {n_in + len(new_shapes): 0, n_in + len(new_shapes) + 1: 1}
    if not first:
        aliases.update({n_in + i: 2 + i for i in range(len(new_shapes))})
    outs = pl.pallas_call(
        functools.partial(_sample_kernel, cfg=cfg),
        grid=(n_req,),
        in_specs=in_specs,
        out_specs=out_specs,
        out_shape=out_shape,
        input_output_aliases=aliases,
        scratch_shapes=[pltpu.VMEM((t_new, n_heads_b, HEAD_DIM), F32) for _ in range(len(pairs) + 1)],
        compiler_params=_params(1),
        name="sample_step",
    )(*args)
    return outs[0], outs[1], list(outs[2:])


def _rope_tables(positions):
    half = HEAD_DIM // 2
    inv = ROPE_THETA ** (-jnp.arange(half, dtype=F32) / half)
    ang = positions.astype(F32)[:, None] * inv[None, :]
    cos, sin = jnp.cos(ang), jnp.sin(ang)
    return jnp.concatenate([cos, cos], axis=1), jnp.concatenate([-sin, sin], axis=1)


def kernel(x_prompt, x_sample, cache_a_kv, cache_b1_kv, cache_b2_kv, cache_b3_kv, norm_ffn1, w_ffn1_in,
           w_ffn1_out, norm_mix, w_in, sinks, w_gate, w_up_a, w_up_b, w_o, norm_ffn2, w_ffn2_in,
           w_ffn2_out, norm_final):
    batch, seq, d_model = x_prompt.shape
    n_req, t_new, _ = x_sample.shape
    depth = w_in.shape[0]
    n_heads_a = sinks.shape[1]
    n_kv_a = cache_a_kv.shape[4]
    grp_a = n_heads_a // n_kv_a
    n_heads_b = cache_b1_kv.shape[4]
    w_qa, w_ka, w_b = n_heads_a * HEAD_DIM, n_kv_a * HEAD_DIM, n_heads_b * HEAD_DIM
    in_width = w_in.shape[2]
    assert in_width == w_qa + 2 * w_ka + 9 * w_b
    caches_b = (cache_b1_kv, cache_b2_kv, cache_b3_kv)

    n_p, n_s = batch * seq, n_req * t_new
    n_tok = n_p + n_s
    tm = n_tok // 8
    assert n_tok % (8 * 32) == 0

    x = jnp.concatenate([x_prompt.reshape(n_p, d_model), x_sample.reshape(n_s, d_model)], axis=0)
    positions = jnp.concatenate([jnp.tile(jnp.arange(seq), batch),
                                 jnp.tile(PAST_LEN + jnp.arange(t_new), n_req)])
    cos_t, sin_t = _rope_tables(positions)

    tn_qkv = 1024
    seg = [(w_qa, True), (w_ka, True), (w_ka, False)] + [(w_b, True), (w_b, True), (w_b, False)] * 3
    rope_flags = tuple(int(f) for w, f in seg for _ in range(w // HEAD_DIM))
    assert len(rope_flags) == in_width // HEAD_DIM
    col_b = tuple(w_qa + 2 * w_ka + 3 * w_b * g for g in range(3))

    new_p = [None] * (1 + len(DILATED_PAIRS))
    new_s = None
    tm_norm = n_s
    for l in range(depth):
        h = _rmsnorm(x, norm_ffn1[l], BF16, tm_norm)
        x = _ffn(x, h, w_ffn1_in, w_ffn1_out, l, tm)

        h = _rmsnorm(x, norm_mix[l], BF16, tm_norm)
        qkv = _matmul([h], [(w_in, l, 0, 0)],
                      [(cos_t, (tm, HEAD_DIM), lambda j, i: (i, 0)), (sin_t, (tm, HEAD_DIM), lambda j, i: (i, 0))],
                      functools.partial(_rope_epilogue, tn=tn_qkv, rope_flags=rope_flags),
                      in_width, F32, tm, tn_qkv, "qkv_rope", n_split=2)

        o_a = _swa_prompt(qkv, batch, seq, n_kv_a, grp_a, WINDOW_A - 1, sinks[l])
        o_b = _dilated_prompt(qkv, batch, seq, col_b, n_heads_b, DILATED_PAIRS)
        o_a, o_b, new_s = _sample_step(qkv, n_p, cache_a_kv, caches_b, sinks[l], new_s, l,
                                       DILATED_PAIRS, col_b, t_new, o_a, o_b)
        new_p[0] = _export_window(qkv, new_p[0], l, depth, batch, seq, min(WINDOW_A, seq), w_qa, n_kv_a)
        for g, (window, dil) in enumerate(DILATED_PAIRS):
            new_p[1 + g] = _export_window(qkv, new_p[1 + g], l, depth, batch, seq, min(window, seq),
                                          col_b[g] + w_b, n_heads_b)
        tn_g = 256
        u = _matmul([h, o_a, o_b],
                    [(w_gate, l, 0, 0), (w_gate, l, 0, d_model // tn_g), (w_up_a, l, 1, 0), (w_up_b, l, 2, 0)],
                    [], _gate_epilogue, d_model, BF16, tm, tn_g, "gate_up")
        x = _matmul([u], [(w_o, l, 0, 0)], [(x, (tm, 512), lambda j, i: (i, j))],
                    functools.partial(_residual_epilogue, scale=1.0), d_model, F32, tm, 512, "w_o", n_split=2)

        h = _rmsnorm(x, norm_ffn2[l], BF16, tm_norm)
        x = _ffn(x, h, w_ffn2_in, w_ffn2_out, l, tm)

    y_prompt = _rmsnorm(x, norm_final, F32, tm_norm, 0, n_p).reshape(batch, seq, d_model)
    y_sample = _rmsnorm(x, norm_final, F32, tm_norm, n_p, n_s).reshape(n_req, t_new, d_model)
    return (y_prompt, y_sample, new_p[0], new_s[0], new_p[1], new_s[1], new_p[2], new_s[2], new_p[3], new_s[3])
```

```python
import functools

import jax
import jax.numpy as jnp
from jax import lax
from jax.experimental import pallas as pl
from jax.experimental.pallas import tpu as pltpu

HEAD_DIM = 128
BLOCK = 128
WINDOW_A = 128
PAST_LEN = 16384
DILATED_PAIRS = ((128, 1), (512, 4), (2048, 16))
ROPE_THETA = 10000.0
EPS = 1e-5
NEG = -1e30
V7X_VMEM_BYTES = 64 * 1024 * 1024
VMEM_LIMIT = V7X_VMEM_BYTES - 8 * 1024 * 1024
UNITS_IN_FLIGHT = 8

F32 = jnp.float32
BF16 = jnp.bfloat16


def _params(n_axes, vmem=VMEM_LIMIT):
    return pltpu.CompilerParams(dimension_semantics=("arbitrary",) * n_axes, vmem_limit_bytes=vmem)


def _dot(a, b):
    return jnp.dot(a, b, preferred_element_type=F32)


def _dot_nt(a, b):
    return lax.dot_general(a, b, (((1,), (1,)), ((), ())), preferred_element_type=F32)


def _head(h):
    return slice(h * HEAD_DIM, (h + 1) * HEAD_DIM)


def _rmsnorm_kernel(x_ref, g_ref, o_ref):
    x = x_ref[...]
    y = x * lax.rsqrt(jnp.mean(x * x, axis=-1, keepdims=True) + EPS)
    o_ref[...] = (y * g_ref[...]).astype(o_ref.dtype)


def _rmsnorm(x, g, out_dtype, tm, row0=0, rows=None):
    d = x.shape[1]
    rows = x.shape[0] if rows is None else rows
    assert rows % tm == 0 and row0 % tm == 0
    blk0 = row0 // tm
    return pl.pallas_call(
        _rmsnorm_kernel,
        grid=(rows // tm,),
        in_specs=[pl.BlockSpec((tm, d), lambda i: (blk0 + i, 0)), pl.BlockSpec((1, d), lambda i: (0, 0))],
        out_specs=pl.BlockSpec((tm, d), lambda i: (i, 0)),
        out_shape=jax.ShapeDtypeStruct((rows, d), out_dtype),
        compiler_params=_params(1),
        name="rmsnorm",
    )(x, g.reshape(1, d))


def _mm_kernel(*refs, n_a, b_to_a, b_is_f32, n_extra, has_side, epilogue, n_split, lead):
    n_b = len(b_to_a)
    a_refs = refs[:n_a]
    b_refs = refs[n_a:n_a + n_b]
    pos = n_a + n_b
    extra_refs = refs[pos:pos + n_extra]
    pos += n_extra
    side_in = refs[pos] if has_side else None
    pos += int(has_side)
    o_ref = refs[pos]
    side_out = refs[pos + 1] if has_side else None
    scratch = list(refs[pos + 1 + int(has_side):])
    w_refs = [scratch.pop(0) if f32 else b_ref for b_ref, f32 in zip(b_refs, b_is_f32)]
    jj, i = pl.program_id(0), pl.program_id(1)

    if has_side:
        side_out[...] = side_in[...].astype(BF16)

    def fill():
        for b_ref, w_ref, f32 in zip(b_refs, w_refs, b_is_f32):
            if f32:
                kc = b_ref.shape[0]
                w_ref[jj % 2, pl.ds(pl.multiple_of(i * kc, kc), kc), :] = b_ref[...].astype(BF16)

    def compute():
        ws = [w_ref[(jj + 1) % 2] if f32 else w_ref[...] for w_ref, f32 in zip(w_refs, b_is_f32)]
        rows_per = o_ref.shape[0] // n_split
        for s in range(n_split):
            rows = pl.ds(s * rows_per, rows_per)
            accs = [_dot(a_refs[ai][rows, :], ws[k]) for k, ai in enumerate(b_to_a)]
            out = epilogue(accs, [e[rows, :] for e in extra_refs], jj - lead)
            o_ref[rows, :] = out.astype(o_ref.dtype)

    if lead:
        pl.when(jj == 0)(fill)

        @pl.when(jj > 0)
        def _():
            fill()
            compute()
    else:
        compute()


def _matmul(a_list, b_list, extras, epilogue, n_out, out_dtype, tm, tn, name, n_split=1, side=None):
    m = a_list[0].shape[0]
    n_j, n_i = n_out // tn, m // tm
    assert m % tm == 0 and n_out % tn == 0 and tm % (16 * n_split) == 0
    lead = int(any(b[0].ndim == 3 for b in b_list))

    def col(jj):
        return jnp.maximum(jj - lead, 0)

    def row(jj, i):
        return jnp.where(jj < lead, 0, i)

    in_specs = [pl.BlockSpec((tm, a.shape[1]), lambda jj, i: (row(jj, i), 0)) for a in a_list]
    args = list(a_list)
    scratch = []
    for w, layer, ai, off in b_list:
        k = a_list[ai].shape[1]
        if w.ndim == 3:
            assert w.shape[1] == k and w.shape[2] % tn == 0 and k % (16 * n_i) == 0
            in_specs.append(pl.BlockSpec(
                (None, k // n_i, tn),
                lambda jj, i, layer=layer, off=off: (layer, i, jnp.minimum(jj, n_j - 1) + off)))
            args.append(w)
            scratch.append(pltpu.VMEM((2, k, tn), BF16))
        else:
            assert w.dtype == BF16 and w.shape[0] == k and w.shape[1] % tn == 0
            in_specs.append(pl.BlockSpec((k, tn), lambda jj, i, off=off: (0, col(jj) + off)))
            args.append(w)
    for arr, shape, imap in extras:
        assert shape[0] == tm
        in_specs.append(pl.BlockSpec(shape, lambda jj, i, imap=imap: imap(col(jj), row(jj, i))))
        args.append(arr)
    out_shape = [jax.ShapeDtypeStruct((m, n_out), out_dtype)]
    out_specs = [pl.BlockSpec((tm, tn), lambda jj, i: (row(jj, i), col(jj)))]
    if side is not None:
        w_side, layer = side
        _, r, c = w_side.shape
        rs = r // (n_j * n_i)
        assert r % (n_j * n_i) == 0 and rs % 16 == 0
        in_specs.append(pl.BlockSpec((None, rs, c), lambda jj, i: (layer, col(jj) * n_i + row(jj, i), 0)))
        out_shape.append(jax.ShapeDtypeStruct((r, c), BF16))
        out_specs.append(pl.BlockSpec((rs, c), lambda jj, i: (col(jj) * n_i + row(jj, i), 0)))
        args.append(w_side)
    kernel = functools.partial(_mm_kernel, n_a=len(a_list), b_to_a=tuple(b[2] for b in b_list),
                               b_is_f32=tuple(b[0].ndim == 3 for b in b_list),
                               n_extra=len(extras), has_side=side is not None, epilogue=epilogue,
                               n_split=n_split, lead=lead)
    outs = pl.pallas_call(
        kernel,
        grid=(n_j + lead, m // tm),
        in_specs=in_specs,
        out_specs=out_specs,
        out_shape=out_shape,
        scratch_shapes=scratch,
        compiler_params=_params(2),
        name=name,
    )(*args)
    return outs[0] if side is None else outs


def _swiglu_epilogue(accs, extras, j):
    a, b = accs
    return a * jax.nn.sigmoid(a) * b


def _residual_epilogue(accs, extras, j, *, scale):
    return extras[0] + scale * accs[0]


def _gate_epilogue(accs, extras, j):
    g_a, g_b, u_a, u_b = accs
    return jax.nn.sigmoid(g_a) * u_a + jax.nn.sigmoid(g_b) * u_b


def _rope_epilogue(accs, extras, j, *, tn, rope_flags):
    acc = accs[0]
    cos, sin = extras
    per_tile = tn // HEAD_DIM
    n_tiles = len(rope_flags) // per_tile
    assert n_tiles < 31 and len(rope_flags) == n_tiles * per_tile
    heads = []
    for h in range(per_tile):
        flag_bits = sum(int(rope_flags[t * per_tile + h]) << t for t in range(n_tiles))
        bit = lax.shift_right_logical(jnp.int32(flag_bits), j) & 1
        on = jnp.broadcast_to(bit, (1, HEAD_DIM)).astype(F32)
        x = acc[:, _head(h)]
        heads.append(x * (on * cos + (1.0 - on)) + pltpu.roll(x, HEAD_DIM // 2, 1) * (on * sin))
    return jnp.concatenate(heads, axis=1)


def _ffn(x, h, w_in, w_out, layer, tm):
    tf = 256
    d_ff = w_out.shape[1]
    assert d_ff % tf == 0
    g, w_out_bf16 = _matmul([h], [(w_in, layer, 0, 0), (w_in, layer, 0, d_ff // tf)], [], _swiglu_epilogue,
                            d_ff, BF16, 2 * tm, tf, "ffn_in", n_split=2, side=(w_out, layer))
    tn = 512
    return _matmul([g], [(w_out_bf16, 0, 0, 0)], [(x, (tm // 2, tn), lambda j, i: (i, j))],
                   functools.partial(_residual_epilogue, scale=0.5), x.shape[1], F32, tm // 2, tn, "ffn_out")


def _softmax_pv(s, valid, v, sink=None):
    s = jnp.where(valid, s, NEG)
    m = jnp.max(s, axis=-1, keepdims=True)
    if sink is not None:
        m = jnp.maximum(m, sink)
    e = jnp.exp(s - m)
    l = jnp.sum(e, axis=-1, keepdims=True)
    denom = l if sink is None else l + jnp.exp(sink - m)
    return _dot(e.astype(BF16), v) / denom, m + jnp.log(l)


def _swa_prompt_kernel(q_ref, kc_ref, kp_ref, vc_ref, vp_ref, sink_ref, o_ref, *, n_kv, grp, max_dist):
    assert max_dist < BLOCK
    rows = grp * BLOCK
    scale = HEAD_DIM ** -0.5
    prev_on = jnp.where(pl.program_id(1) == 0, 0, 1)
    qi = lax.broadcasted_iota(jnp.int32, (rows, BLOCK), 0) & (BLOCK - 1)
    kj = lax.broadcasted_iota(jnp.int32, (rows, BLOCK), 1)
    tri = kj <= qi
    valid = (jnp.where(tri, qi - kj, qi + BLOCK - kj) <= max_dist) & (kj <= qi + prev_on * BLOCK)

    for kh in range(n_kv):
        q = jnp.concatenate([q_ref[:, _head(kh * grp + g)] for g in range(grp)], axis=0).astype(BF16)
        s_cur = _dot_nt(q, kc_ref[:, _head(kh)].astype(BF16))
        s_prev = _dot_nt(q, kp_ref[:, _head(kh)].astype(BF16))
        s = jnp.where(valid, jnp.where(tri, s_cur, s_prev) * scale, NEG)
        sink = jnp.concatenate(
            [jnp.full((BLOCK, 1), sink_ref[kh * grp + g], F32) for g in range(grp)], axis=0)
        m = jnp.maximum(jnp.max(s, axis=-1, keepdims=True), sink)
        e = jnp.exp(s - m)
        denom = jnp.sum(e, axis=-1, keepdims=True) + jnp.exp(sink - m)
        o = (_dot(jnp.where(tri, e, 0.0).astype(BF16), vc_ref[:, _head(kh)].astype(BF16))
             + _dot(jnp.where(tri, 0.0, e).astype(BF16), vp_ref[:, _head(kh)].astype(BF16))) / denom
        for g in range(grp):
            o_ref[:, _head(kh * grp + g)] = o[g * BLOCK:(g + 1) * BLOCK].astype(o_ref.dtype)


def _swa_prompt(qkv, n_batch, seq, n_kv, grp, max_dist, sink):
    wq, wk = n_kv * grp * HEAD_DIM, n_kv * HEAD_DIM
    assert seq % BLOCK == 0 and wq % wk == 0
    nbl = seq // BLOCK
    k_blk, v_blk = wq // wk, wq // wk + 1

    def imap(col, prev):
        return lambda n, b: (n * nbl + (jnp.maximum(b - 1, 0) if prev else b), col)

    return pl.pallas_call(
        functools.partial(_swa_prompt_kernel, n_kv=n_kv, grp=grp, max_dist=max_dist),
        grid=(n_batch, nbl),
        in_specs=[pl.BlockSpec((BLOCK, wq), imap(0, False)),
                  pl.BlockSpec((BLOCK, wk), imap(k_blk, False)), pl.BlockSpec((BLOCK, wk), imap(k_blk, True)),
                  pl.BlockSpec((BLOCK, wk), imap(v_blk, False)), pl.BlockSpec((BLOCK, wk), imap(v_blk, True)),
                  pl.BlockSpec(memory_space=pltpu.SMEM)],
        out_specs=pl.BlockSpec((BLOCK, wq), lambda n, b: (n * nbl + b, 0)),
        out_shape=jax.ShapeDtypeStruct((qkv.shape[0], wq), BF16),
        compiler_params=_params(2),
        name="swa_prompt",
    )(qkv, qkv, qkv, qkv, qkv, sink)


def _dilated_prompt_kernel(*refs, groups, seq):
    n_g = len(groups)
    qkv_refs = refs[:3 * n_g]
    o_ref = refs[3 * n_g]
    o_scr = refs[3 * n_g + 1:3 * n_g + 1 + n_g]
    l_scr = refs[3 * n_g + 1 + n_g:]
    scale = HEAD_DIM ** -0.5

    def band(n_keys, off, max_dist):
        qi = lax.broadcasted_iota(jnp.int32, (BLOCK, n_keys), 0)
        kj = lax.broadcasted_iota(jnp.int32, (BLOCK, n_keys), 1)
        dist = qi + off - kj
        return (dist >= 0) & (dist <= max_dist)

    for g, (max_dist, dil) in enumerate(groups):
        q_ref, k_ref, v_ref = qkv_refs[3 * g:3 * g + 3]
        valid_first, valid_next = band(BLOCK, 0, max_dist), band(2 * BLOCK, BLOCK, max_dist)

        def rows(start, dil=dil):
            return pl.ds(start, BLOCK, stride=dil) if dil > 1 else pl.ds(start, BLOCK)

        units = [(r, b) for r in range(dil) for b in range(seq // dil // BLOCK)]
        for u0 in range(0, len(units), UNITS_IN_FLIGHT):
            staged = []
            for r, b in units[u0:u0 + UNITS_IN_FLIGHT]:
                cur = rows(r + b * BLOCK * dil)
                q = q_ref[cur, :].astype(BF16)
                if b == 0:
                    k, v, valid = k_ref[cur, :], v_ref[cur, :], valid_first
                else:
                    prev = rows(r + (b - 1) * BLOCK * dil)
                    k = jnp.concatenate([k_ref[prev, :], k_ref[cur, :]], axis=0)
                    v = jnp.concatenate([v_ref[prev, :], v_ref[cur, :]], axis=0)
                    valid = valid_next
                staged.append((cur, jnp.where(valid, _dot_nt(q, k.astype(BF16)) * scale, NEG), v.astype(BF16)))
            probs = []
            for cur, s, v in staged:
                m = jnp.max(s, axis=-1, keepdims=True)
                e = jnp.exp(s - m)
                l = jnp.sum(e, axis=-1, keepdims=True)
                probs.append((cur, e.astype(BF16), v, l, m))
            for cur, e, v, l, m in probs:
                o_scr[g][cur, :] = _dot(e, v) / l
                l_scr[g][cur, :] = jnp.broadcast_to(m + jnp.log(l), (BLOCK, HEAD_DIM))

    chunk = 2 * BLOCK
    for c in range(seq // chunk):
        cs = pl.ds(c * chunk, chunk)
        lses = [l[cs, :] for l in l_scr]
        m = functools.reduce(jnp.maximum, lses)
        ws = [jnp.exp(l - m) for l in lses]
        tot = functools.reduce(jnp.add, ws)
        mix = functools.reduce(jnp.add, [(w / tot) * o[cs, :] for w, o in zip(ws, o_scr)])
        o_ref[cs, :] = mix.astype(o_ref.dtype)


def _dilated_prompt(qkv, n_batch, seq, col0, n_heads, pairs):
    groups = tuple((window // dil, dil) for window, dil in pairs)
    assert all(seq % (dil * BLOCK) == 0 and window // dil <= BLOCK for window, dil in pairs)
    w = n_heads * HEAD_DIM
    in_specs = []
    for c in col0:
        assert c % HEAD_DIM == 0
        for part in range(3):
            cb = (c + part * w) // HEAD_DIM
            in_specs.append(pl.BlockSpec((seq, HEAD_DIM), lambda n, h, cb=cb: (n, cb + h)))
    n_g = len(groups)
    return pl.pallas_call(
        functools.partial(_dilated_prompt_kernel, groups=groups, seq=seq),
        grid=(n_batch, n_heads),
        in_specs=in_specs,
        out_specs=pl.BlockSpec((seq, HEAD_DIM), lambda n, h: (n, h)),
        out_shape=jax.ShapeDtypeStruct((qkv.shape[0], w), BF16),
        scratch_shapes=[pltpu.VMEM((seq, HEAD_DIM), F32) for _ in range(2 * n_g)],
        compiler_params=_params(2),
        name="dilated_prompt",
    )(*([qkv] * (3 * n_g)))


def _export_kernel(k_ref, v_ref, prev_ref, o_ref, *, n_kv):
    del prev_ref
    for h in range(n_kv):
        o_ref[:, 0, h, :] = k_ref[:, _head(h)]
        o_ref[:, 1, h, :] = v_ref[:, _head(h)]


def _export_window(qkv, prev, layer, depth, n_batch, seq, lw, k_col, n_kv):
    wk = n_kv * HEAD_DIM
    rb = min(lw, 512)
    assert lw % rb == 0 and seq % rb == 0 and k_col % wk == 0
    shape = (depth, n_batch, lw, 2, n_kv, HEAD_DIM)
    first = prev is None
    if first:
        prev = jnp.zeros((8, HEAD_DIM), F32)
    row_blk0 = (seq - lw) // rb

    def imap(part):
        return lambda n, c: (n * (seq // rb) + row_blk0 + c, k_col // wk + part)

    return pl.pallas_call(
        functools.partial(_export_kernel, n_kv=n_kv),
        grid=(n_batch, lw // rb),
        in_specs=[pl.BlockSpec((rb, wk), imap(0)), pl.BlockSpec((rb, wk), imap(1)),
                  pl.BlockSpec(memory_space=pl.ANY)],
        out_specs=pl.BlockSpec((None, None, rb, 2, n_kv, HEAD_DIM), lambda n, c: (layer, n, c, 0, 0, 0)),
        out_shape=jax.ShapeDtypeStruct(shape, F32),
        input_output_aliases={} if first else {2: 0},
        compiler_params=_params(2),
        name="export_window",
    )(qkv, qkv, prev)


def _sample_group(q_ref, q_col, new_ref, k_col, c_ref, *, n_kv, grp, dil, lb, max_dist, sink, t_new):
    n_q = n_kv * grp
    wk = n_kv * HEAD_DIM
    rows = n_q * t_new
    n_cache = lb // dil
    n_keys = n_cache + BLOCK
    scale = HEAD_DIM ** -0.5
    assert dil & (dil - 1) == 0 and t_new & (t_new - 1) == 0

    def lane_block(x, idx, n):
        zero = jnp.zeros_like(x)
        return jnp.concatenate([x if i == idx else zero for i in range(n)], axis=1)

    q_bd = jnp.concatenate(
        [lane_block(q_ref[:, q_col + h * HEAD_DIM:q_col + (h + 1) * HEAD_DIM], h // grp, n_kv)
         for h in range(n_q)], axis=0).astype(BF16)
    pad = jnp.zeros((BLOCK - t_new, wk), F32)
    k_new = jnp.concatenate([new_ref[:, k_col:k_col + wk], pad], axis=0)
    v_new = jnp.concatenate([new_ref[:, k_col + wk:k_col + 2 * wk], pad], axis=0)

    t_row = lax.broadcasted_iota(jnp.int32, (rows, n_keys), 0) & (t_new - 1)
    lane = lax.broadcasted_iota(jnp.int32, (rows, n_keys), 1)
    t_col = lax.broadcasted_iota(jnp.int32, (rows, 1), 0) & (t_new - 1)

    o_all = jnp.zeros((rows, wk), F32)
    lse_all = jnp.zeros((rows, 1), F32)
    for r in range(min(dil, t_new)):
        k_c = jnp.concatenate([c_ref[:, 2 * r, h, :] for h in range(n_kv)], axis=1)
        v_c = jnp.concatenate([c_ref[:, 2 * r + 1, h, :] for h in range(n_kv)], axis=1)
        k = jnp.concatenate([k_c, k_new], axis=0).astype(BF16)
        v = jnp.concatenate([v_c, v_new], axis=0).astype(BF16)
        delta = jnp.where(lane < n_cache, lb + t_row - (r + dil * lane), t_row - (lane - n_cache))
        valid = ((delta >= 0) & (delta <= max_dist) & ((delta & (dil - 1)) == 0)
                 & (lane < n_cache + t_new))
        o, lse = _softmax_pv(_dot_nt(q_bd, k) * scale, valid, v, sink)
        mine = ((t_col + lb) & (dil - 1)) == r
        o_all = jnp.where(mine, o, o_all)
        lse_all = jnp.where(mine, lse, lse_all)
    return o_all, lse_all


def _sample_dilated_group(q_ref, new_ref, c_ref, *, n_heads, dil, lb, max_dist, t_new):
    assert n_heads == 8 and dil & (dil - 1) == 0
    h_bits = n_heads.bit_length() - 1
    n_cache = lb // dil
    n_c, n_n = n_cache * n_heads, t_new * n_heads
    scale = HEAD_DIM ** -0.5
    q_all = q_ref[...].reshape(n_n, HEAD_DIM)
    k_new = new_ref[:, 0, :, :].reshape(n_n, HEAD_DIM).astype(BF16)
    v_new = new_ref[:, 1, :, :].reshape(n_n, HEAD_DIM).astype(BF16)
    outs, lses = [None] * t_new, [None] * t_new
    scores = []
    for r in range(min(dil, t_new)):
        ts = [t for t in range(t_new) if (lb + t) % dil == r]
        rows = n_heads * len(ts)
        q = jnp.concatenate([q_all[t * n_heads:(t + 1) * n_heads] for t in ts], axis=0).astype(BF16)
        k_c = c_ref[:, 2 * r, :, :].reshape(n_c, HEAD_DIM).astype(BF16)

        def mask(n_cols, key_row, rows=rows, ts=ts):
            row = lax.broadcasted_iota(jnp.int32, (rows, n_cols), 0)
            col = lax.broadcasted_iota(jnp.int32, (rows, n_cols), 1)
            t_q = ts[0] + dil * (row >> h_bits)
            delta = lb + t_q - key_row(col >> h_bits)
            same_head = (row & (n_heads - 1)) == (col & (n_heads - 1))
            return same_head & (delta >= 0) & (delta <= max_dist) & ((delta & (dil - 1)) == 0)

        s_c = jnp.where(mask(n_c, lambda k, r=r: r + dil * k), _dot_nt(q, k_c) * scale, NEG)
        s_n = jnp.where(mask(n_n, lambda t: lb + t), _dot_nt(q, k_new) * scale, NEG)
        scores.append((r, ts, s_c, s_n))
    probs = []
    for r, ts, s_c, s_n in scores:
        m = jnp.maximum(jnp.max(s_c, axis=-1, keepdims=True), jnp.max(s_n, axis=-1, keepdims=True))
        e_c, e_n = jnp.exp(s_c - m), jnp.exp(s_n - m)
        l = jnp.sum(e_c, axis=-1, keepdims=True) + jnp.sum(e_n, axis=-1, keepdims=True)
        probs.append((r, ts, e_c.astype(BF16), e_n.astype(BF16), l, m + jnp.log(l)))
    for r, ts, e_c, e_n, l, lse in probs:
        v_c = c_ref[:, 2 * r + 1, :, :].reshape(n_c, HEAD_DIM).astype(BF16)
        o = (_dot(e_c, v_c) + _dot(e_n, v_new)) / l
        for i, t in enumerate(ts):
            outs[t] = o[i * n_heads:(i + 1) * n_heads]
            lses[t] = lse[i * n_heads:(i + 1) * n_heads]
    return jnp.concatenate(outs, axis=0), jnp.concatenate(lses, axis=0)


def _sample_kernel(*refs, cfg):
    t_new, n_kv_a, grp_a, n_heads_b, pairs, lbs, lb_a, col_b = (
        cfg["t_new"], cfg["n_kv_a"], cfg["grp_a"], cfg["n_heads_b"], cfg["pairs"], cfg["lbs"],
        cfg["lb_a"], cfg["col_b"])
    n_g = len(pairs)
    qkv_ref, ca_ref = refs[0], refs[1]
    cb_refs = refs[2:2 + n_g]
    sink_ref = refs[2 + n_g]
    pos = 3 + n_g + (1 + n_g) + 2
    oa_ref, ob_ref, na_ref = refs[pos], refs[pos + 1], refs[pos + 2]
    nb_refs = refs[pos + 3:pos + 3 + n_g]
    q_scrs = refs[pos + 3 + n_g:pos + 3 + 2 * n_g]
    mix_scr = refs[pos + 3 + 2 * n_g]

    n_q_a = n_kv_a * grp_a
    w_qa, w_ka, w_b = n_q_a * HEAD_DIM, n_kv_a * HEAD_DIM, n_heads_b * HEAD_DIM

    sink = jnp.concatenate([jnp.full((t_new, 1), sink_ref[h], F32) for h in range(n_q_a)], axis=0)
    o, _ = _sample_group(qkv_ref, 0, qkv_ref, w_qa, ca_ref, n_kv=n_kv_a, grp=grp_a, dil=1, lb=lb_a,
                         max_dist=WINDOW_A - 1, sink=sink, t_new=t_new)
    for h in range(n_q_a):
        oa_ref[:, _head(h)] = o[h * t_new:(h + 1) * t_new, _head(h // grp_a)].astype(oa_ref.dtype)
    for h in range(n_kv_a):
        na_ref[:, 0, h, :] = qkv_ref[:, w_qa + h * HEAD_DIM:w_qa + (h + 1) * HEAD_DIM]
        na_ref[:, 1, h, :] = qkv_ref[:, w_qa + w_ka + h * HEAD_DIM:w_qa + w_ka + (h + 1) * HEAD_DIM]

    outs, lses = [], []
    for g, (window, dil) in enumerate(pairs):
        c0 = col_b[g]
        for h in range(n_heads_b):
            nb_refs[g][:, 0, h, :] = qkv_ref[:, c0 + w_b + h * HEAD_DIM:c0 + w_b + (h + 1) * HEAD_DIM]
            nb_refs[g][:, 1, h, :] = qkv_ref[:, c0 + 2 * w_b + h * HEAD_DIM:c0 + 2 * w_b + (h + 1) * HEAD_DIM]
            q_scrs[g][:, h, :] = qkv_ref[:, c0 + h * HEAD_DIM:c0 + (h + 1) * HEAD_DIM]
        o, lse = _sample_dilated_group(q_scrs[g], nb_refs[g], cb_refs[g], n_heads=n_heads_b, dil=dil,
                                       lb=lbs[g], max_dist=window, t_new=t_new)
        outs.append(o)
        lses.append(lse)
    m = functools.reduce(jnp.maximum, lses)
    ws = [jnp.exp(l - m) for l in lses]
    tot = functools.reduce(jnp.add, ws)
    mix = functools.reduce(jnp.add, [(w / tot) * o for w, o in zip(ws, outs)])
    mix_scr[...] = mix.reshape(t_new, n_heads_b, HEAD_DIM)
    for h in range(n_heads_b):
        ob_ref[:, _head(h)] = mix_scr[:, h, :].astype(ob_ref.dtype)


def _sample_step(qkv, row0, cache_a, caches_b, sink, prev_new, layer, pairs, col_b, t_new, o_a, o_b):
    depth, n_req, lb_a, _, n_kv_a, _ = cache_a.shape
    n_q_a = sink.shape[0]
    n_heads_b = caches_b[0].shape[4]
    in_width = qkv.shape[1]
    assert row0 % t_new == 0 and t_new % 8 == 0
    lbs = tuple(c.shape[2] for c in caches_b)
    cfg = dict(t_new=t_new, n_kv_a=n_kv_a, grp_a=n_q_a // n_kv_a, n_heads_b=n_heads_b, pairs=pairs,
               lbs=lbs, lb_a=lb_a, col_b=col_b)

    def cache_spec(cache, lb, dil):
        n_kv = cache.shape[4]
        assert lb % dil == 0 and lb // dil == BLOCK
        view = cache.reshape(depth, n_req, lb // dil, dil * 2, n_kv, HEAD_DIM)
        n_cls = min(dil, t_new)
        return view, pl.BlockSpec((None, None, lb // dil, n_cls * 2, n_kv, HEAD_DIM),
                                  lambda b: (layer, b, 0, 0, 0, 0))

    args = [qkv]
    in_specs = [pl.BlockSpec((t_new, in_width), lambda b: (row0 // t_new + b, 0))]
    view, spec = cache_spec(cache_a, lb_a, 1)
    args.append(view)
    in_specs.append(spec)
    for cache, lb, (window, dil) in zip(caches_b, lbs, pairs):
        view, spec = cache_spec(cache, lb, dil)
        args.append(view)
        in_specs.append(spec)
    args.append(sink)
    in_specs.append(pl.BlockSpec(memory_space=pltpu.SMEM))

    new_shapes = [(depth, n_req, t_new, 2, n_kv_a, HEAD_DIM)] + [(depth, n_req, t_new, 2, n_heads_b, HEAD_DIM)] * len(pairs)
    first = prev_new is None
    if first:
        prev_new = [jnp.zeros((8, HEAD_DIM), F32) for _ in new_shapes]
    n_in = len(args)
    args += list(prev_new)
    in_specs += [pl.BlockSpec(memory_space=pl.ANY)] * len(prev_new)

    assert o_a.shape == (qkv.shape[0], n_q_a * HEAD_DIM) and o_b.shape == (qkv.shape[0], n_heads_b * HEAD_DIM)
    args += [o_a, o_b]
    in_specs += [pl.BlockSpec(memory_space=pl.ANY)] * 2
    out_shape = [jax.ShapeDtypeStruct(o_a.shape, o_a.dtype), jax.ShapeDtypeStruct(o_b.shape, o_b.dtype)]
    out_specs = [pl.BlockSpec((t_new, o_a.shape[1]), lambda b: (row0 // t_new + b, 0)),
                 pl.BlockSpec((t_new, o_b.shape[1]), lambda b: (row0 // t_new + b, 0))]
    for s in new_shapes:
        out_shape.append(jax.ShapeDtypeStruct(s, F32))
        out_specs.append(pl.BlockSpec((None, None) + s[2:], lambda b: (layer, b, 0, 0, 0, 0)))
    aliases = {n_in + len(new_shapes): 0, n_in + len(new_shapes) + 1: 1}
    if not first:
        aliases.update({n_in + i: 2 + i for i in range(len(new_shapes))})
    outs = pl.pallas_call(
        functools.partial(_sample_kernel, cfg=cfg),
        grid=(n_req,),
        in_specs=in_specs,
        out_specs=out_specs,
        out_shape=out_shape,
        input_output_aliases=aliases,
        scratch_shapes=[pltpu.VMEM((t_new, n_heads_b, HEAD_DIM), F32) for _ in range(len(pairs) + 1)],
        compiler_params=_params(1),
        name="sample_step",
    )(*args)
    return outs[0], outs[1], list(outs[2:])


def _rope_tables(positions):
    half = HEAD_DIM // 2
    inv = ROPE_THETA ** (-jnp.arange(half, dtype=F32) / half)
    ang = positions.astype(F32)[:, None] * inv[None, :]
    cos, sin = jnp.cos(ang), jnp.sin(ang)
    return jnp.concatenate([cos, cos], axis=1), jnp.concatenate([-sin, sin], axis=1)


def kernel(x_prompt, x_sample, cache_a_kv, cache_b1_kv, cache_b2_kv, cache_b3_kv, norm_ffn1, w_ffn1_in,
           w_ffn1_out, norm_mix, w_in, sinks, w_gate, w_up_a, w_up_b, w_o, norm_ffn2, w_ffn2_in,
           w_ffn2_out, norm_final):
    batch, seq, d_model = x_prompt.shape
    n_req, t_new, _ = x_sample.shape
    depth = w_in.shape[0]
    n_heads_a = sinks.shape[1]
    n_kv_a = cache_a_kv.shape[4]
    grp_a = n_heads_a // n_kv_a
    n_heads_b = cache_b1_kv.shape[4]
    w_qa, w_ka, w_b = n_heads_a * HEAD_DIM, n_kv_a * HEAD_DIM, n_heads_b * HEAD_DIM
    in_width = w_in.shape[2]
    assert in_width == w_qa + 2 * w_ka + 9 * w_b
    caches_b = (cache_b1_kv, cache_b2_kv, cache_b3_kv)

    n_p, n_s = batch * seq, n_req * t_new
    n_tok = n_p + n_s
    tm = n_tok // 8
    assert n_tok % (8 * 32) == 0

    x = jnp.concatenate([x_prompt.reshape(n_p, d_model), x_sample.reshape(n_s, d_model)], axis=0)
    positions = jnp.concatenate([jnp.tile(jnp.arange(seq), batch),
                                 jnp.tile(PAST_LEN + jnp.arange(t_new), n_req)])
    cos_t, sin_t = _rope_tables(positions)

    tn_qkv = 1024
    seg = [(w_qa, True), (w_ka, True), (w_ka, False)] + [(w_b, True), (w_b, True), (w_b, False)] * 3
    rope_flags = tuple(int(f) for w, f in seg for _ in range(w // HEAD_DIM))
    assert len(rope_flags) == in_width // HEAD_DIM
    col_b = tuple(w_qa + 2 * w_ka + 3 * w_b * g for g in range(3))

    new_p = [None] * (1 + len(DILATED_PAIRS))
    new_s = None
    tm_norm = tm // 2
    for l in range(depth):
        h = _rmsnorm(x, norm_ffn1[l], BF16, tm_norm)
        x = _ffn(x, h, w_ffn1_in, w_ffn1_out, l, tm)

        h = _rmsnorm(x, norm_mix[l], BF16, tm_norm)
        qkv = _matmul([h], [(w_in, l, 0, 0)],
                      [(cos_t, (tm, HEAD_DIM), lambda j, i: (i, 0)), (sin_t, (tm, HEAD_DIM), lambda j, i: (i, 0))],
                      functools.partial(_rope_epilogue, tn=tn_qkv, rope_flags=rope_flags),
                      in_width, F32, tm, tn_qkv, "qkv_rope", n_split=2)

        o_a = _swa_prompt(qkv, batch, seq, n_kv_a, grp_a, WINDOW_A - 1, sinks[l])
        o_b = _dilated_prompt(qkv, batch, seq, col_b, n_heads_b, DILATED_PAIRS)
        o_a, o_b, new_s = _sample_step(qkv, n_p, cache_a_kv, caches_b, sinks[l], new_s, l,
                                       DILATED_PAIRS, col_b, t_new, o_a, o_b)
        new_p[0] = _export_window(qkv, new_p[0], l, depth, batch, seq, min(WINDOW_A, seq), w_qa, n_kv_a)
        for g, (window, dil) in enumerate(DILATED_PAIRS):
            new_p[1 + g] = _export_window(qkv, new_p[1 + g], l, depth, batch, seq, min(window, seq),
                                          col_b[g] + w_b, n_heads_b)
        tn_g = 256
        u = _matmul([h, o_a, o_b],
                    [(w_gate, l, 0, 0), (w_gate, l, 0, d_model // tn_g), (w_up_a, l, 1, 0), (w_up_b, l, 2, 0)],
                    [], _gate_epilogue, d_model, BF16, tm, tn_g, "gate_up")
        x = _matmul([u], [(w_o, l, 0, 0)], [(x, (tm, 512), lambda j, i: (i, j))],
                    functools.partial(_residual_epilogue, scale=1.0), d_model, F32, tm, 512, "w_o", n_split=2)

        h = _rmsnorm(x, norm_ffn2[l], BF16, tm_norm)
        x = _ffn(x, h, w_ffn2_in, w_ffn2_out, l, tm)

    y_prompt = _rmsnorm(x, norm_final, F32, 2 * n_s, 0, n_p).reshape(batch, seq, d_model)
    y_sample = _rmsnorm(x, norm_final, F32, n_s, n_p, n_s).reshape(n_req, t_new, d_model)
    return (y_prompt, y_sample, new_p[0], new_s[0], new_p[1], new_s[1], new_p[2], new_s[2], new_p[3], new_s[3])
```

```python
import functools

import jax
import jax.numpy as jnp
from jax import lax
from jax.experimental import pallas as pl
from jax.experimental.pallas import tpu as pltpu

HEAD_DIM = 128
BLOCK = 128
WINDOW_A = 128
PAST_LEN = 16384
DILATED_PAIRS = ((128, 1), (512, 4), (2048, 16))
ROPE_THETA = 10000.0
EPS = 1e-5
NEG = -1e30
V7X_VMEM_BYTES = 64 * 1024 * 1024
VMEM_LIMIT = V7X_VMEM_BYTES - 8 * 1024 * 1024
UNITS_IN_FLIGHT = 8

F32 = jnp.float32
BF16 = jnp.bfloat16


def _params(n_axes, vmem=VMEM_LIMIT):
    return pltpu.CompilerParams(dimension_semantics=("arbitrary",) * n_axes, vmem_limit_bytes=vmem)


def _dot(a, b):
    return jnp.dot(a, b, preferred_element_type=F32)


def _dot_nt(a, b):
    return lax.dot_general(a, b, (((1,), (1,)), ((), ())), preferred_element_type=F32)


def _head(h):
    return slice(h * HEAD_DIM, (h + 1) * HEAD_DIM)


def _rmsnorm_kernel(x_ref, g_ref, o_ref):
    x = x_ref[...]
    y = x * lax.rsqrt(jnp.mean(x * x, axis=-1, keepdims=True) + EPS)
    o_ref[...] = (y * g_ref[...]).astype(o_ref.dtype)


def _rmsnorm(x, g, out_dtype, tm, row0=0, rows=None):
    d = x.shape[1]
    rows = x.shape[0] if rows is None else rows
    assert rows % tm == 0 and row0 % tm == 0
    blk0 = row0 // tm
    return pl.pallas_call(
        _rmsnorm_kernel,
        grid=(rows // tm,),
        in_specs=[pl.BlockSpec((tm, d), lambda i: (blk0 + i, 0)), pl.BlockSpec((1, d), lambda i: (0, 0))],
        out_specs=pl.BlockSpec((tm, d), lambda i: (i, 0)),
        out_shape=jax.ShapeDtypeStruct((rows, d), out_dtype),
        compiler_params=_params(1),
        name="rmsnorm",
    )(x, g.reshape(1, d))


def _stack_norm_kernel(xp_ref, xs_ref, g_ref, x_ref, h_ref, *, n_prompt_tiles):
    def emit(x):
        x_ref[...] = x
        y = x * lax.rsqrt(jnp.mean(x * x, axis=-1, keepdims=True) + EPS)
        h_ref[...] = (y * g_ref[...]).astype(h_ref.dtype)

    pl.when(pl.program_id(0) < n_prompt_tiles)(lambda: emit(xp_ref[...]))
    pl.when(pl.program_id(0) >= n_prompt_tiles)(lambda: emit(xs_ref[...]))


def _stack_norm(x_prompt, x_sample, g, tm):
    n_p, d = x_prompt.shape
    n_s = x_sample.shape[0]
    assert n_p % tm == 0 and n_s % tm == 0
    n_pt = n_p // tm
    tile = pl.BlockSpec((tm, d), lambda i: (i, 0))
    return pl.pallas_call(
        functools.partial(_stack_norm_kernel, n_prompt_tiles=n_pt),
        grid=((n_p + n_s) // tm,),
        in_specs=[pl.BlockSpec((tm, d), lambda i: (jnp.minimum(i, n_pt - 1), 0)),
                  pl.BlockSpec((tm, d), lambda i: (jnp.maximum(i - n_pt, 0), 0)),
                  pl.BlockSpec((1, d), lambda i: (0, 0))],
        out_specs=[tile, tile],
        out_shape=[jax.ShapeDtypeStruct((n_p + n_s, d), F32), jax.ShapeDtypeStruct((n_p + n_s, d), BF16)],
        compiler_params=_params(1),
        name="stack_norm",
    )(x_prompt, x_sample, g.reshape(1, d))


def _mm_kernel(*refs, n_a, b_to_a, b_is_f32, n_extra, has_side, epilogue, n_split, lead):
    n_b = len(b_to_a)
    a_refs = refs[:n_a]
    b_refs = refs[n_a:n_a + n_b]
    pos = n_a + n_b
    extra_refs = refs[pos:pos + n_extra]
    pos += n_extra
    side_in = refs[pos] if has_side else None
    pos += int(has_side)
    o_ref = refs[pos]
    side_out = refs[pos + 1] if has_side else None
    scratch = list(refs[pos + 1 + int(has_side):])
    w_refs = [scratch.pop(0) if f32 else b_ref for b_ref, f32 in zip(b_refs, b_is_f32)]
    jj, i = pl.program_id(0), pl.program_id(1)

    if has_side:
        side_out[...] = side_in[...].astype(BF16)

    def fill():
        for b_ref, w_ref, f32 in zip(b_refs, w_refs, b_is_f32):
            if f32:
                kc = b_ref.shape[0]
                w_ref[jj % 2, pl.ds(pl.multiple_of(i * kc, kc), kc), :] = b_ref[...].astype(BF16)

    def compute():
        ws = [w_ref[(jj + 1) % 2] if f32 else w_ref[...] for w_ref, f32 in zip(w_refs, b_is_f32)]
        rows_per = o_ref.shape[0] // n_split
        for s in range(n_split):
            rows = pl.ds(s * rows_per, rows_per)
            accs = [_dot(a_refs[ai][rows, :], ws[k]) for k, ai in enumerate(b_to_a)]
            out = epilogue(accs, [e[rows, :] for e in extra_refs], jj - lead)
            o_ref[rows, :] = out.astype(o_ref.dtype)

    if lead:
        pl.when(jj == 0)(fill)

        @pl.when(jj > 0)
        def _():
            fill()
            compute()
    else:
        compute()


def _matmul(a_list, b_list, extras, epilogue, n_out, out_dtype, tm, tn, name, n_split=1, side=None):
    m = a_list[0].shape[0]
    n_j, n_i = n_out // tn, m // tm
    assert m % tm == 0 and n_out % tn == 0 and tm % (16 * n_split) == 0
    lead = int(any(b[0].ndim == 3 for b in b_list))

    def col(jj):
        return jnp.maximum(jj - lead, 0)

    def row(jj, i):
        return jnp.where(jj < lead, 0, i)

    in_specs = [pl.BlockSpec((tm, a.shape[1]), lambda jj, i: (row(jj, i), 0)) for a in a_list]
    args = list(a_list)
    scratch = []
    for w, layer, ai, off in b_list:
        k = a_list[ai].shape[1]
        if w.ndim == 3:
            assert w.shape[1] == k and w.shape[2] % tn == 0 and k % (16 * n_i) == 0
            in_specs.append(pl.BlockSpec(
                (None, k // n_i, tn),
                lambda jj, i, layer=layer, off=off: (layer, i, jnp.minimum(jj, n_j - 1) + off)))
            args.append(w)
            scratch.append(pltpu.VMEM((2, k, tn), BF16))
        else:
            assert w.dtype == BF16 and w.shape[0] == k and w.shape[1] % tn == 0
            in_specs.append(pl.BlockSpec((k, tn), lambda jj, i, off=off: (0, col(jj) + off)))
            args.append(w)
    for arr, shape, imap in extras:
        assert shape[0] == tm
        in_specs.append(pl.BlockSpec(shape, lambda jj, i, imap=imap: imap(col(jj), row(jj, i))))
        args.append(arr)
    out_shape = [jax.ShapeDtypeStruct((m, n_out), out_dtype)]
    out_specs = [pl.BlockSpec((tm, tn), lambda jj, i: (row(jj, i), col(jj)))]
    if side is not None:
        w_side, layer = side
        _, r, c = w_side.shape
        rs = r // (n_j * n_i)
        assert r % (n_j * n_i) == 0 and rs % 16 == 0
        in_specs.append(pl.BlockSpec((None, rs, c), lambda jj, i: (layer, col(jj) * n_i + row(jj, i), 0)))
        out_shape.append(jax.ShapeDtypeStruct((r, c), BF16))
        out_specs.append(pl.BlockSpec((rs, c), lambda jj, i: (col(jj) * n_i + row(jj, i), 0)))
        args.append(w_side)
    kernel = functools.partial(_mm_kernel, n_a=len(a_list), b_to_a=tuple(b[2] for b in b_list),
                               b_is_f32=tuple(b[0].ndim == 3 for b in b_list),
                               n_extra=len(extras), has_side=side is not None, epilogue=epilogue,
                               n_split=n_split, lead=lead)
    outs = pl.pallas_call(
        kernel,
        grid=(n_j + lead, m // tm),
        in_specs=in_specs,
        out_specs=out_specs,
        out_shape=out_shape,
        scratch_shapes=scratch,
        compiler_params=_params(2),
        name=name,
    )(*args)
    return outs[0] if side is None else outs


def _swiglu_epilogue(accs, extras, j):
    a, b = accs
    return a * jax.nn.sigmoid(a) * b


def _residual_epilogue(accs, extras, j, *, scale):
    return extras[0] + scale * accs[0]


def _gate_epilogue(accs, extras, j):
    g_a, g_b, u_a, u_b = accs
    return jax.nn.sigmoid(g_a) * u_a + jax.nn.sigmoid(g_b) * u_b


def _rope_epilogue(accs, extras, j, *, tn, rope_flags):
    acc = accs[0]
    cos, sin = extras
    per_tile = tn // HEAD_DIM
    n_tiles = len(rope_flags) // per_tile
    assert n_tiles < 31 and len(rope_flags) == n_tiles * per_tile
    heads = []
    for h in range(per_tile):
        flag_bits = sum(int(rope_flags[t * per_tile + h]) << t for t in range(n_tiles))
        bit = lax.shift_right_logical(jnp.int32(flag_bits), j) & 1
        on = jnp.broadcast_to(bit, (1, HEAD_DIM)).astype(F32)
        x = acc[:, _head(h)]
        heads.append(x * (on * cos + (1.0 - on)) + pltpu.roll(x, HEAD_DIM // 2, 1) * (on * sin))
    return jnp.concatenate(heads, axis=1)


def _ffn(x, h, w_in, w_out, layer, tm):
    tf = 256
    d_ff = w_out.shape[1]
    assert d_ff % tf == 0
    g, w_out_bf16 = _matmul([h], [(w_in, layer, 0, 0), (w_in, layer, 0, d_ff // tf)], [], _swiglu_epilogue,
                            d_ff, BF16, 2 * tm, tf, "ffn_in", n_split=2, side=(w_out, layer))
    tn = 512
    return _matmul([g], [(w_out_bf16, 0, 0, 0)], [(x, (tm // 2, tn), lambda j, i: (i, j))],
                   functools.partial(_residual_epilogue, scale=0.5), x.shape[1], F32, tm // 2, tn, "ffn_out")


def _softmax_pv(s, valid, v, sink=None):
    s = jnp.where(valid, s, NEG)
    m = jnp.max(s, axis=-1, keepdims=True)
    if sink is not None:
        m = jnp.maximum(m, sink)
    e = jnp.exp(s - m)
    l = jnp.sum(e, axis=-1, keepdims=True)
    denom = l if sink is None else l + jnp.exp(sink - m)
    return _dot(e.astype(BF16), v) / denom, m + jnp.log(l)


def _swa_prompt_kernel(q_ref, kc_ref, kp_ref, vc_ref, vp_ref, sink_ref, o_ref, *, n_kv, grp, max_dist):
    assert max_dist < BLOCK
    rows = grp * BLOCK
    scale = HEAD_DIM ** -0.5
    prev_on = jnp.where(pl.program_id(1) == 0, 0, 1)
    qi = lax.broadcasted_iota(jnp.int32, (rows, BLOCK), 0) & (BLOCK - 1)
    kj = lax.broadcasted_iota(jnp.int32, (rows, BLOCK), 1)
    tri = kj <= qi
    valid = (jnp.where(tri, qi - kj, qi + BLOCK - kj) <= max_dist) & (kj <= qi + prev_on * BLOCK)

    for kh in range(n_kv):
        q = jnp.concatenate([q_ref[:, _head(kh * grp + g)] for g in range(grp)], axis=0).astype(BF16)
        s_cur = _dot_nt(q, kc_ref[:, _head(kh)].astype(BF16))
        s_prev = _dot_nt(q, kp_ref[:, _head(kh)].astype(BF16))
        s = jnp.where(valid, jnp.where(tri, s_cur, s_prev) * scale, NEG)
        sink = jnp.concatenate(
            [jnp.full((BLOCK, 1), sink_ref[kh * grp + g], F32) for g in range(grp)], axis=0)
        m = jnp.maximum(jnp.max(s, axis=-1, keepdims=True), sink)
        e = jnp.exp(s - m)
        denom = jnp.sum(e, axis=-1, keepdims=True) + jnp.exp(sink - m)
        o = (_dot(jnp.where(tri, e, 0.0).astype(BF16), vc_ref[:, _head(kh)].astype(BF16))
             + _dot(jnp.where(tri, 0.0, e).astype(BF16), vp_ref[:, _head(kh)].astype(BF16))) / denom
        for g in range(grp):
            o_ref[:, _head(kh * grp + g)] = o[g * BLOCK:(g + 1) * BLOCK].astype(o_ref.dtype)


def _swa_prompt(qkv, n_batch, seq, n_kv, grp, max_dist, sink):
    wq, wk = n_kv * grp * HEAD_DIM, n_kv * HEAD_DIM
    assert seq % BLOCK == 0 and wq % wk == 0
    nbl = seq // BLOCK
    k_blk, v_blk = wq // wk, wq // wk + 1

    def imap(col, prev):
        return lambda n, b: (n * nbl + (jnp.maximum(b - 1, 0) if prev else b), col)

    return pl.pallas_call(
        functools.partial(_swa_prompt_kernel, n_kv=n_kv, grp=grp, max_dist=max_dist),
        grid=(n_batch, nbl),
        in_specs=[pl.BlockSpec((BLOCK, wq), imap(0, False)),
                  pl.BlockSpec((BLOCK, wk), imap(k_blk, False)), pl.BlockSpec((BLOCK, wk), imap(k_blk, True)),
                  pl.BlockSpec((BLOCK, wk), imap(v_blk, False)), pl.BlockSpec((BLOCK, wk), imap(v_blk, True)),
                  pl.BlockSpec(memory_space=pltpu.SMEM)],
        out_specs=pl.BlockSpec((BLOCK, wq), lambda n, b: (n * nbl + b, 0)),
        out_shape=jax.ShapeDtypeStruct((qkv.shape[0], wq), BF16),
        compiler_params=_params(2),
        name="swa_prompt",
    )(qkv, qkv, qkv, qkv, qkv, sink)


def _dilated_prompt_kernel(*refs, groups, seq):
    n_g = len(groups)
    qkv_refs = refs[:3 * n_g]
    o_ref = refs[3 * n_g]
    o_scr = refs[3 * n_g + 1:3 * n_g + 1 + n_g]
    l_scr = refs[3 * n_g + 1 + n_g:]
    scale = HEAD_DIM ** -0.5

    def band(n_keys, off, max_dist):
        qi = lax.broadcasted_iota(jnp.int32, (BLOCK, n_keys), 0)
        kj = lax.broadcasted_iota(jnp.int32, (BLOCK, n_keys), 1)
        dist = qi + off - kj
        return (dist >= 0) & (dist <= max_dist)

    for g, (max_dist, dil) in enumerate(groups):
        q_ref, k_ref, v_ref = qkv_refs[3 * g:3 * g + 3]
        valid_first, valid_next = band(BLOCK, 0, max_dist), band(2 * BLOCK, BLOCK, max_dist)

        def rows(start, dil=dil):
            return pl.ds(start, BLOCK, stride=dil) if dil > 1 else pl.ds(start, BLOCK)

        units = [(r, b) for r in range(dil) for b in range(seq // dil // BLOCK)]
        for u0 in range(0, len(units), UNITS_IN_FLIGHT):
            staged = []
            for r, b in units[u0:u0 + UNITS_IN_FLIGHT]:
                cur = rows(r + b * BLOCK * dil)
                q = q_ref[cur, :].astype(BF16)
                if b == 0:
                    k, v, valid = k_ref[cur, :], v_ref[cur, :], valid_first
                else:
                    prev = rows(r + (b - 1) * BLOCK * dil)
                    k = jnp.concatenate([k_ref[prev, :], k_ref[cur, :]], axis=0)
                    v = jnp.concatenate([v_ref[prev, :], v_ref[cur, :]], axis=0)
                    valid = valid_next
                staged.append((cur, jnp.where(valid, _dot_nt(q, k.astype(BF16)) * scale, NEG), v.astype(BF16)))
            probs = []
            for cur, s, v in staged:
                m = jnp.max(s, axis=-1, keepdims=True)
                e = jnp.exp(s - m)
                l = jnp.sum(e, axis=-1, keepdims=True)
                probs.append((cur, e.astype(BF16), v, l, m))
            for cur, e, v, l, m in probs:
                o_scr[g][cur, :] = _dot(e, v) / l
                l_scr[g][cur, :] = jnp.broadcast_to(m + jnp.log(l), (BLOCK, HEAD_DIM))

    chunk = 2 * BLOCK
    for c in range(seq // chunk):
        cs = pl.ds(c * chunk, chunk)
        lses = [l[cs, :] for l in l_scr]
        m = functools.reduce(jnp.maximum, lses)
        ws = [jnp.exp(l - m) for l in lses]
        tot = functools.reduce(jnp.add, ws)
        mix = functools.reduce(jnp.add, [(w / tot) * o[cs, :] for w, o in zip(ws, o_scr)])
        o_ref[cs, :] = mix.astype(o_ref.dtype)


def _dilated_prompt(qkv, n_batch, seq, col0, n_heads, pairs):
    groups = tuple((window // dil, dil) for window, dil in pairs)
    assert all(seq % (dil * BLOCK) == 0 and window // dil <= BLOCK for window, dil in pairs)
    w = n_heads * HEAD_DIM
    in_specs = []
    for c in col0:
        assert c % HEAD_DIM == 0
        for part in range(3):
            cb = (c + part * w) // HEAD_DIM
            in_specs.append(pl.BlockSpec((seq, HEAD_DIM), lambda n, h, cb=cb: (n, cb + h)))
    n_g = len(groups)
    return pl.pallas_call(
        functools.partial(_dilated_prompt_kernel, groups=groups, seq=seq),
        grid=(n_batch, n_heads),
        in_specs=in_specs,
        out_specs=pl.BlockSpec((seq, HEAD_DIM), lambda n, h: (n, h)),
        out_shape=jax.ShapeDtypeStruct((qkv.shape[0], w), BF16),
        scratch_shapes=[pltpu.VMEM((seq, HEAD_DIM), F32) for _ in range(2 * n_g)],
        compiler_params=_params(2),
        name="dilated_prompt",
    )(*([qkv] * (3 * n_g)))


def _export_kernel(k_ref, v_ref, prev_ref, o_ref, *, n_kv):
    del prev_ref
    for h in range(n_kv):
        o_ref[:, 0, h, :] = k_ref[:, _head(h)]
        o_ref[:, 1, h, :] = v_ref[:, _head(h)]


def _export_window(qkv, prev, layer, depth, n_batch, seq, lw, k_col, n_kv):
    wk = n_kv * HEAD_DIM
    rb = min(lw, 512)
    assert lw % rb == 0 and seq % rb == 0 and k_col % wk == 0
    shape = (depth, n_batch, lw, 2, n_kv, HEAD_DIM)
    first = prev is None
    if first:
        prev = jnp.zeros((8, HEAD_DIM), F32)
    row_blk0 = (seq - lw) // rb

    def imap(part):
        return lambda n, c: (n * (seq // rb) + row_blk0 + c, k_col // wk + part)

    return pl.pallas_call(
        functools.partial(_export_kernel, n_kv=n_kv),
        grid=(n_batch, lw // rb),
        in_specs=[pl.BlockSpec((rb, wk), imap(0)), pl.BlockSpec((rb, wk), imap(1)),
                  pl.BlockSpec(memory_space=pl.ANY)],
        out_specs=pl.BlockSpec((None, None, rb, 2, n_kv, HEAD_DIM), lambda n, c: (layer, n, c, 0, 0, 0)),
        out_shape=jax.ShapeDtypeStruct(shape, F32),
        input_output_aliases={} if first else {2: 0},
        compiler_params=_params(2),
        name="export_window",
    )(qkv, qkv, prev)


def _sample_group(q_ref, q_col, new_ref, k_col, c_ref, *, n_kv, grp, dil, lb, max_dist, sink, t_new):
    n_q = n_kv * grp
    wk = n_kv * HEAD_DIM
    rows = n_q * t_new
    n_cache = lb // dil
    n_keys = n_cache + BLOCK
    scale = HEAD_DIM ** -0.5
    assert dil & (dil - 1) == 0 and t_new & (t_new - 1) == 0

    def lane_block(x, idx, n):
        zero = jnp.zeros_like(x)
        return jnp.concatenate([x if i == idx else zero for i in range(n)], axis=1)

    q_bd = jnp.concatenate(
        [lane_block(q_ref[:, q_col + h * HEAD_DIM:q_col + (h + 1) * HEAD_DIM], h // grp, n_kv)
         for h in range(n_q)], axis=0).astype(BF16)
    pad = jnp.zeros((BLOCK - t_new, wk), F32)
    k_new = jnp.concatenate([new_ref[:, k_col:k_col + wk], pad], axis=0)
    v_new = jnp.concatenate([new_ref[:, k_col + wk:k_col + 2 * wk], pad], axis=0)

    t_row = lax.broadcasted_iota(jnp.int32, (rows, n_keys), 0) & (t_new - 1)
    lane = lax.broadcasted_iota(jnp.int32, (rows, n_keys), 1)
    t_col = lax.broadcasted_iota(jnp.int32, (rows, 1), 0) & (t_new - 1)

    o_all = jnp.zeros((rows, wk), F32)
    lse_all = jnp.zeros((rows, 1), F32)
    for r in range(min(dil, t_new)):
        k_c = jnp.concatenate([c_ref[:, 2 * r, h, :] for h in range(n_kv)], axis=1)
        v_c = jnp.concatenate([c_ref[:, 2 * r + 1, h, :] for h in range(n_kv)], axis=1)
        k = jnp.concatenate([k_c, k_new], axis=0).astype(BF16)
        v = jnp.concatenate([v_c, v_new], axis=0).astype(BF16)
        delta = jnp.where(lane < n_cache, lb + t_row - (r + dil * lane), t_row - (lane - n_cache))
        valid = ((delta >= 0) & (delta <= max_dist) & ((delta & (dil - 1)) == 0)
                 & (lane < n_cache + t_new))
        o, lse = _softmax_pv(_dot_nt(q_bd, k) * scale, valid, v, sink)
        mine = ((t_col + lb) & (dil - 1)) == r
        o_all = jnp.where(mine, o, o_all)
        lse_all = jnp.where(mine, lse, lse_all)
    return o_all, lse_all


def _sample_dilated_group(q_ref, new_ref, c_ref, *, n_heads, dil, lb, max_dist, t_new):
    assert n_heads == 8 and dil & (dil - 1) == 0
    h_bits = n_heads.bit_length() - 1
    n_cache = lb // dil
    n_c, n_n = n_cache * n_heads, t_new * n_heads
    scale = HEAD_DIM ** -0.5
    q_all = q_ref[...].reshape(n_n, HEAD_DIM)
    k_new = new_ref[:, 0, :, :].reshape(n_n, HEAD_DIM).astype(BF16)
    v_new = new_ref[:, 1, :, :].reshape(n_n, HEAD_DIM).astype(BF16)
    outs, lses = [None] * t_new, [None] * t_new
    scores = []
    for r in range(min(dil, t_new)):
        ts = [t for t in range(t_new) if (lb + t) % dil == r]
        rows = n_heads * len(ts)
        q = jnp.concatenate([q_all[t * n_heads:(t + 1) * n_heads] for t in ts], axis=0).astype(BF16)
        k_c = c_ref[:, 2 * r, :, :].reshape(n_c, HEAD_DIM).astype(BF16)

        def mask(n_cols, key_row, rows=rows, ts=ts):
            row = lax.broadcasted_iota(jnp.int32, (rows, n_cols), 0)
            col = lax.broadcasted_iota(jnp.int32, (rows, n_cols), 1)
            t_q = ts[0] + dil * (row >> h_bits)
            delta = lb + t_q - key_row(col >> h_bits)
            same_head = (row & (n_heads - 1)) == (col & (n_heads - 1))
            return same_head & (delta >= 0) & (delta <= max_dist) & ((delta & (dil - 1)) == 0)

        s_c = jnp.where(mask(n_c, lambda k, r=r: r + dil * k), _dot_nt(q, k_c) * scale, NEG)
        s_n = jnp.where(mask(n_n, lambda t: lb + t), _dot_nt(q, k_new) * scale, NEG)
        scores.append((r, ts, s_c, s_n))
    probs = []
    for r, ts, s_c, s_n in scores:
        m = jnp.maximum(jnp.max(s_c, axis=-1, keepdims=True), jnp.max(s_n, axis=-1, keepdims=True))
        e_c, e_n = jnp.exp(s_c - m), jnp.exp(s_n - m)
        l = jnp.sum(e_c, axis=-1, keepdims=True) + jnp.sum(e_n, axis=-1, keepdims=True)
        probs.append((r, ts, e_c.astype(BF16), e_n.astype(BF16), l, m + jnp.log(l)))
    for r, ts, e_c, e_n, l, lse in probs:
        v_c = c_ref[:, 2 * r + 1, :, :].reshape(n_c, HEAD_DIM).astype(BF16)
        o = (_dot(e_c, v_c) + _dot(e_n, v_new)) / l
        for i, t in enumerate(ts):
            outs[t] = o[i * n_heads:(i + 1) * n_heads]
            lses[t] = lse[i * n_heads:(i + 1) * n_heads]
    return jnp.concatenate(outs, axis=0), jnp.concatenate(lses, axis=0)


def _sample_kernel(*refs, cfg):
    t_new, n_kv_a, grp_a, n_heads_b, pairs, lbs, lb_a, col_b = (
        cfg["t_new"], cfg["n_kv_a"], cfg["grp_a"], cfg["n_heads_b"], cfg["pairs"], cfg["lbs"],
        cfg["lb_a"], cfg["col_b"])
    n_g = len(pairs)
    qkv_ref, ca_ref = refs[0], refs[1]
    cb_refs = refs[2:2 + n_g]
    sink_ref = refs[2 + n_g]
    pos = 3 + n_g + (1 + n_g) + 2
    oa_ref, ob_ref, na_ref = refs[pos], refs[pos + 1], refs[pos + 2]
    nb_refs = refs[pos + 3:pos + 3 + n_g]
    q_scrs = refs[pos + 3 + n_g:pos + 3 + 2 * n_g]
    mix_scr = refs[pos + 3 + 2 * n_g]

    n_q_a = n_kv_a * grp_a
    w_qa, w_ka, w_b = n_q_a * HEAD_DIM, n_kv_a * HEAD_DIM, n_heads_b * HEAD_DIM

    sink = jnp.concatenate([jnp.full((t_new, 1), sink_ref[h], F32) for h in range(n_q_a)], axis=0)
    o, _ = _sample_group(qkv_ref, 0, qkv_ref, w_qa, ca_ref, n_kv=n_kv_a, grp=grp_a, dil=1, lb=lb_a,
                         max_dist=WINDOW_A - 1, sink=sink, t_new=t_new)
    for h in range(n_q_a):
        oa_ref[:, _head(h)] = o[h * t_new:(h + 1) * t_new, _head(h // grp_a)].astype(oa_ref.dtype)
    for h in range(n_kv_a):
        na_ref[:, 0, h, :] = qkv_ref[:, w_qa + h * HEAD_DIM:w_qa + (h + 1) * HEAD_DIM]
        na_ref[:, 1, h, :] = qkv_ref[:, w_qa + w_ka + h * HEAD_DIM:w_qa + w_ka + (h + 1) * HEAD_DIM]

    outs, lses = [], []
    for g, (window, dil) in enumerate(pairs):
        c0 = col_b[g]
        for h in range(n_heads_b):
            nb_refs[g][:, 0, h, :] = qkv_ref[:, c0 + w_b + h * HEAD_DIM:c0 + w_b + (h + 1) * HEAD_DIM]
            nb_refs[g][:, 1, h, :] = qkv_ref[:, c0 + 2 * w_b + h * HEAD_DIM:c0 + 2 * w_b + (h + 1) * HEAD_DIM]
            q_scrs[g][:, h, :] = qkv_ref[:, c0 + h * HEAD_DIM:c0 + (h + 1) * HEAD_DIM]
        o, lse = _sample_dilated_group(q_scrs[g], nb_refs[g], cb_refs[g], n_heads=n_heads_b, dil=dil,
                                       lb=lbs[g], max_dist=window, t_new=t_new)
        outs.append(o)
        lses.append(lse)
    m = functools.reduce(jnp.maximum, lses)
    ws = [jnp.exp(l - m) for l in lses]
    tot = functools.reduce(jnp.add, ws)
    mix = functools.reduce(jnp.add, [(w / tot) * o for w, o in zip(ws, outs)])
    mix_scr[...] = mix.reshape(t_new, n_heads_b, HEAD_DIM)
    for h in range(n_heads_b):
        ob_ref[:, _head(h)] = mix_scr[:, h, :].astype(ob_ref.dtype)


def _sample_step(qkv, row0, cache_a, caches_b, sink, prev_new, layer, pairs, col_b, t_new, o_a, o_b):
    depth, n_req, lb_a, _, n_kv_a, _ = cache_a.shape
    n_q_a = sink.shape[0]
    n_heads_b = caches_b[0].shape[4]
    in_width = qkv.shape[1]
    assert row0 % t_new == 0 and t_new % 8 == 0
    lbs = tuple(c.shape[2] for c in caches_b)
    cfg = dict(t_new=t_new, n_kv_a=n_kv_a, grp_a=n_q_a // n_kv_a, n_heads_b=n_heads_b, pairs=pairs,
               lbs=lbs, lb_a=lb_a, col_b=col_b)

    def cache_spec(cache, lb, dil):
        n_kv = cache.shape[4]
        assert lb % dil == 0 and lb // dil == BLOCK
        view = cache.reshape(depth, n_req, lb // dil, dil * 2, n_kv, HEAD_DIM)
        n_cls = min(dil, t_new)
        return view, pl.BlockSpec((None, None, lb // dil, n_cls * 2, n_kv, HEAD_DIM),
                                  lambda b: (layer, b, 0, 0, 0, 0))

    args = [qkv]
    in_specs = [pl.BlockSpec((t_new, in_width), lambda b: (row0 // t_new + b, 0))]
    view, spec = cache_spec(cache_a, lb_a, 1)
    args.append(view)
    in_specs.append(spec)
    for cache, lb, (window, dil) in zip(caches_b, lbs, pairs):
        view, spec = cache_spec(cache, lb, dil)
        args.append(view)
        in_specs.append(spec)
    args.append(sink)
    in_specs.append(pl.BlockSpec(memory_space=pltpu.SMEM))

    new_shapes = [(depth, n_req, t_new, 2, n_kv_a, HEAD_DIM)] + [(depth, n_req, t_new, 2, n_heads_b, HEAD_DIM)] * len(pairs)
    first = prev_new is None
    if first:
        prev_new = [jnp.zeros((8, HEAD_DIM), F32) for _ in new_shapes]
    n_in = len(args)
    args += list(prev_new)
    in_specs += [pl.BlockSpec(memory_space=pl.ANY)] * len(prev_new)

    assert o_a.shape == (qkv.shape[0], n_q_a * HEAD_DIM) and o_b.shape == (qkv.shape[0], n_heads_b * HEAD_DIM)
    args += [o_a, o_b]
    in_specs += [pl.BlockSpec(memory_space=pl.ANY)] * 2
    out_shape = [jax.ShapeDtypeStruct(o_a.shape, o_a.dtype), jax.ShapeDtypeStruct(o_b.shape, o_b.dtype)]
    out_specs = [pl.BlockSpec((t_new, o_a.shape[1]), lambda b: (row0 // t_new + b, 0)),
                 pl.BlockSpec((t_new, o_b.shape[1]), lambda b: (row0 // t_new + b, 0))]
    for s in new_shapes:
        out_shape.append(jax.ShapeDtypeStruct(s, F32))
        out_specs.append(pl.BlockSpec((None, None) + s[2:], lambda b: (layer, b, 0, 0, 0, 0)))
    aliases = {n_in + len(new_shapes): 0, n_in + len(new_shapes) + 1: 1}
    if not first:
        aliases.update({n_in + i: 2 + i for i in range(len(new_shapes))})
    outs = pl.pallas_call(
        functools.partial(_sample_kernel, cfg=cfg),
        grid=(n_req,),
        in_specs=in_specs,
        out_specs=out_specs,
        out_shape=out_shape,
        input_output_aliases=aliases,
        scratch_shapes=[pltpu.VMEM((t_new, n_heads_b, HEAD_DIM), F32) for _ in range(len(pairs) + 1)],
        compiler_params=_params(1),
        name="sample_step",
    )(*args)
    return outs[0], outs[1], list(outs[2:])


def _rope_tables(positions):
    half = HEAD_DIM // 2
    inv = ROPE_THETA ** (-jnp.arange(half, dtype=F32) / half)
    ang = positions.astype(F32)[:, None] * inv[None, :]
    cos, sin = jnp.cos(ang), jnp.sin(ang)
    return jnp.concatenate([cos, cos], axis=1), jnp.concatenate([-sin, sin], axis=1)


def kernel(x_prompt, x_sample, cache_a_kv, cache_b1_kv, cache_b2_kv, cache_b3_kv, norm_ffn1, w_ffn1_in,
           w_ffn1_out, norm_mix, w_in, sinks, w_gate, w_up_a, w_up_b, w_o, norm_ffn2, w_ffn2_in,
           w_ffn2_out, norm_final):
    batch, seq, d_model = x_prompt.shape
    n_req, t_new, _ = x_sample.shape
    depth = w_in.shape[0]
    n_heads_a = sinks.shape[1]
    n_kv_a = cache_a_kv.shape[4]
    grp_a = n_heads_a // n_kv_a
    n_heads_b = cache_b1_kv.shape[4]
    w_qa, w_ka, w_b = n_heads_a * HEAD_DIM, n_kv_a * HEAD_DIM, n_heads_b * HEAD_DIM
    in_width = w_in.shape[2]
    assert in_width == w_qa + 2 * w_ka + 9 * w_b
    caches_b = (cache_b1_kv, cache_b2_kv, cache_b3_kv)

    n_p, n_s = batch * seq, n_req * t_new
    n_tok = n_p + n_s
    tm = n_tok // 8
    assert n_tok % (8 * 32) == 0

    assert n_p % n_s == 0
    x, h = _stack_norm(x_prompt.reshape(n_p, d_model), x_sample.reshape(n_s, d_model), norm_ffn1[0], n_s)
    positions = jnp.concatenate([jnp.tile(jnp.arange(seq), batch),
                                 jnp.tile(PAST_LEN + jnp.arange(t_new), n_req)])
    cos_t, sin_t = _rope_tables(positions)

    tn_qkv = 1024
    seg = [(w_qa, True), (w_ka, True), (w_ka, False)] + [(w_b, True), (w_b, True), (w_b, False)] * 3
    rope_flags = tuple(int(f) for w, f in seg for _ in range(w // HEAD_DIM))
    assert len(rope_flags) == in_width // HEAD_DIM
    col_b = tuple(w_qa + 2 * w_ka + 3 * w_b * g for g in range(3))

    new_p = [None] * (1 + len(DILATED_PAIRS))
    new_s = None
    tm_norm = tm // 2
    for l in range(depth):
        if l > 0:
            h = _rmsnorm(x, norm_ffn1[l], BF16, tm_norm)
        x = _ffn(x, h, w_ffn1_in, w_ffn1_out, l, tm)

        h = _rmsnorm(x, norm_mix[l], BF16, tm_norm)
        qkv = _matmul([h], [(w_in, l, 0, 0)],
                      [(cos_t, (tm, HEAD_DIM), lambda j, i: (i, 0)), (sin_t, (tm, HEAD_DIM), lambda j, i: (i, 0))],
                      functools.partial(_rope_epilogue, tn=tn_qkv, rope_flags=rope_flags),
                      in_width, F32, tm, tn_qkv, "qkv_rope", n_split=2)

        o_a = _swa_prompt(qkv, batch, seq, n_kv_a, grp_a, WINDOW_A - 1, sinks[l])
        o_b = _dilated_prompt(qkv, batch, seq, col_b, n_heads_b, DILATED_PAIRS)
        o_a, o_b, new_s = _sample_step(qkv, n_p, cache_a_kv, caches_b, sinks[l], new_s, l,
                                       DILATED_PAIRS, col_b, t_new, o_a, o_b)
        new_p[0] = _export_window(qkv, new_p[0], l, depth, batch, seq, min(WINDOW_A, seq), w_qa, n_kv_a)
        for g, (window, dil) in enumerate(DILATED_PAIRS):
            new_p[1 + g] = _export_window(qkv, new_p[1 + g], l, depth, batch, seq, min(window, seq),
                                          col_b[g] + w_b, n_heads_b)
        tn_g = 256
        u = _matmul([h, o_a, o_b],
                    [(w_gate, l, 0, 0), (w_gate, l, 0, d_model // tn_g), (w_up_a, l, 1, 0), (w_up_b, l, 2, 0)],
                    [], _gate_epilogue, d_model, BF16, tm, tn_g, "gate_up")
        x = _matmul([u], [(w_o, l, 0, 0)], [(x, (tm, 512), lambda j, i: (i, j))],
                    functools.partial(_residual_epilogue, scale=1.0), d_model, F32, tm, 512, "w_o", n_split=2)

        h = _rmsnorm(x, norm_ffn2[l], BF16, tm_norm)
        x = _ffn(x, h, w_ffn2_in, w_ffn2_out, l, tm)

    y_prompt = _rmsnorm(x, norm_final, F32, 2 * n_s, 0, n_p).reshape(batch, seq, d_model)
    y_sample = _rmsnorm(x, norm_final, F32, n_s, n_p, n_s).reshape(n_req, t_new, d_model)
    return (y_prompt, y_sample, new_p[0], new_s[0], new_p[1], new_s[1], new_p[2], new_s[2], new_p[3], new_s[3])
```
